```python
import jax, jax.numpy as jnp
from jax import lax
import numpy as np

D_MODEL = 1024
BATCH = 1
SEQ = 16384
DEPTH = 1
DEC_BATCH = 32
DEC_SEQ = 4
PAST_LEN = 16384
PAGE_SIZE = 128

N_HEADS = 8
HEAD_DIM = 64
ATTN_W = N_HEADS * HEAD_DIM
CONV_W = D_MODEL
CONV_K = 3
MLP_W = 4 * D_MODEL
MOBA_BLOCK = 256
MOBA_TOPK = 3
Q_BLOCK = 128
ROT_DIM = HEAD_DIM // 4
ROPE_THETA = 500000.0
EPS = 1e-6
NEG = -1e30
IN_W = 3 * ATTN_W + 3 * CONV_W + 2 * D_MODEL

kernel_name = 'hybrid_shortconv_moba_decode_step'


def rmsnorm(x, g):
    xf = x.astype(jnp.float32)
    y = xf * lax.rsqrt(jnp.mean(xf * xf, axis=-1, keepdims=True) + EPS)
    return (y * g.astype(jnp.float32)).astype(x.dtype)


def rope_partial(x, pos):
    half = ROT_DIM // 2
    inv = ROPE_THETA ** (-jnp.arange(half, dtype=jnp.float32) * 2.0 / ROT_DIM)
    ang = pos.astype(jnp.float32)[:, None] * inv[None, :]
    cos = jnp.cos(ang)[None, :, None, :]
    sin = jnp.sin(ang)[None, :, None, :]
    xf = x.astype(jnp.float32)
    x1 = xf[..., :half]
    x2 = xf[..., half:ROT_DIM]
    out = jnp.concatenate([x1 * cos - x2 * sin, x2 * cos + x1 * sin, xf[..., ROT_DIM:]], axis=-1)
    return out.astype(x.dtype)


def project(x, norm_mix, w_in, q_norm, k_norm, pos):
    b, t = x.shape[:2]
    h = rmsnorm(x, norm_mix)
    z = jnp.einsum('btd,de->bte', h, w_in)
    splits = np.cumsum([ATTN_W, ATTN_W, ATTN_W, CONV_W, CONV_W, CONV_W, D_MODEL]).tolist()
    q, k, v, xc, bg, cg, ga, gc = jnp.split(z, splits, axis=-1)
    q = rope_partial(rmsnorm(q.reshape(b, t, N_HEADS, HEAD_DIM), q_norm), pos)
    k = rope_partial(rmsnorm(k.reshape(b, t, N_HEADS, HEAD_DIM), k_norm), pos)
    v = v.reshape(b, t, N_HEADS, HEAD_DIM)
    u = cg * xc
    return q, k, v, u, bg, ga, gc


def short_conv(u, state, conv_w):
    t = u.shape[1]
    up = jnp.concatenate([state.astype(u.dtype), u], axis=1)
    y = conv_w[0] * up[:, 0:t]
    for j in range(1, CONV_K):
        y = y + conv_w[j] * up[:, j:j + t]
    return y, up[:, -(CONV_K - 1):]


def merge_and_mlp(x, attn_o, conv_y, bg, ga, gc, w_br_attn, w_br_conv, w_out, norm_mlp, w_up, w_down):
    b, t = x.shape[:2]
    a = jnp.einsum('bte,ed->btd', attn_o.reshape(b, t, ATTN_W), w_br_attn)
    c = jnp.einsum('bte,ed->btd', bg * conv_y, w_br_conv)
    mixed = jax.nn.sigmoid(ga) * a + jax.nn.sigmoid(gc) * c
    x = x + jnp.einsum('btd,de->bte', mixed, w_out)
    h = rmsnorm(x, norm_mlp)
    hid = jnp.square(jax.nn.relu(jnp.einsum('btd,df->btf', h, w_up)))
    return x + jnp.einsum('btf,fd->btd', hid, w_down)


def moba_core(q, q_pos, k_means, gather, k_loc, v_loc, loc_pos):
    b, nq = q.shape[:2]
    nb = k_means.shape[1]
    if nb < MOBA_TOPK:
        k_means = jnp.pad(k_means, ((0, 0), (0, MOBA_TOPK - nb), (0, 0), (0, 0)))
        nb = MOBA_TOPK
    n_past = q_pos // MOBA_BLOCK
    gate = jnp.einsum('bqhd,bnhd->bqhn', q.astype(jnp.float32), k_means.astype(jnp.float32))
    avail = jnp.arange(nb)[None, :] < n_past[:, None]
    gate = jnp.where(avail[None, :, None, :], gate, NEG)
    _, top_i = lax.top_k(gate, MOBA_TOPK)
    k_sel, v_sel = gather(top_i)
    scale = HEAD_DIM ** -0.5
    l_sel = jnp.einsum('bqhd,bqhskd->bqhsk', q, k_sel, preferred_element_type=jnp.float32) * scale
    sel_ok = jnp.arange(MOBA_TOPK)[None, :] < n_past[:, None]
    l_sel = jnp.where(sel_ok[None, :, None, :, None], l_sel, NEG)
    l_loc = jnp.einsum('bqhd,bkhd->bqhk', q, k_loc, preferred_element_type=jnp.float32) * scale
    loc_ok = ((loc_pos[None, :] // MOBA_BLOCK) == n_past[:, None]) & (loc_pos[None, :] <= q_pos[:, None])
    l_loc = jnp.where(loc_ok[None, :, None, :], l_loc, NEG)
    n_sel = MOBA_TOPK * MOBA_BLOCK
    p = jax.nn.softmax(jnp.concatenate([l_sel.reshape(b, nq, N_HEADS, n_sel), l_loc], axis=-1), axis=-1)
    p_sel = p[..., :n_sel].reshape(b, nq, N_HEADS, MOBA_TOPK, MOBA_BLOCK).astype(v_sel.dtype)
    p_loc = p[..., n_sel:].astype(v_loc.dtype)
    o = (jnp.einsum('bqhsk,bqhskd->bqhd', p_sel, v_sel, preferred_element_type=jnp.float32)
         + jnp.einsum('bqhk,bkhd->bqhd', p_loc, v_loc, preferred_element_type=jnp.float32))
    return o.astype(q.dtype)


def prompt_attention(q, k, v):
    b, t = q.shape[:2]
    n_blk = -(-t // MOBA_BLOCK)
    pad = n_blk * MOBA_BLOCK - t
    k_pad = jnp.pad(k, ((0, 0), (0, pad), (0, 0), (0, 0)))
    v_pad = jnp.pad(v, ((0, 0), (0, pad), (0, 0), (0, 0)))
    kb = k_pad.reshape(b, n_blk, MOBA_BLOCK, N_HEADS, HEAD_DIM)
    vb = v_pad.reshape(b, n_blk, MOBA_BLOCK, N_HEADS, HEAD_DIM)
    k_means = jnp.mean(kb.astype(jnp.float32), axis=2)
    kb_h = kb.transpose(0, 3, 1, 2, 4)
    vb_h = vb.transpose(0, 3, 1, 2, 4)
    bi = jnp.arange(b)[:, None, None, None]
    hi = jnp.arange(N_HEADS)[None, None, :, None]

    def gather(idx):
        idx = jnp.minimum(idx, n_blk - 1)
        return kb_h[bi, hi, idx], vb_h[bi, hi, idx]

    def one_block(c):
        p0 = c * Q_BLOCK
        q_c = lax.dynamic_slice_in_dim(q, p0, Q_BLOCK, axis=1)
        q_pos = p0 + jnp.arange(Q_BLOCK)
        s = (p0 // MOBA_BLOCK) * MOBA_BLOCK
        k_loc = lax.dynamic_slice_in_dim(k_pad, s, MOBA_BLOCK, axis=1)
        v_loc = lax.dynamic_slice_in_dim(v_pad, s, MOBA_BLOCK, axis=1)
        loc_pos = s + jnp.arange(MOBA_BLOCK)
        return moba_core(q_c, q_pos, k_means, gather, k_loc, v_loc, loc_pos)

    out = lax.map(one_block, jnp.arange(t // Q_BLOCK))
    return out.transpose(1, 0, 2, 3, 4).reshape(b, t, N_HEADS, HEAD_DIM)


def sample_attention(q, k_new, v_new, cache_k, cache_v, page_table):
    db, s = q.shape[:2]
    n_pages = page_table.shape[1]
    past = n_pages * PAGE_SIZE
    ppb = MOBA_BLOCK // PAGE_SIZE
    n_full = past // MOBA_BLOCK
    k_past = cache_k[page_table[:, :n_full * ppb]]
    k_means = jnp.mean(k_past.astype(jnp.float32).reshape(db, n_full, MOBA_BLOCK, N_HEADS, HEAD_DIM), axis=2)
    bi = jnp.arange(db)[:, None, None, None, None]
    hi = jnp.arange(N_HEADS)[None, None, :, None, None]

    def gather(idx):
        pidx = jnp.clip(idx[..., None] * ppb + jnp.arange(ppb), 0, n_pages - 1)
        phys = page_table[bi, pidx]
        ks = cache_k[phys, :, hi, :]
        vs = cache_v[phys, :, hi, :]
        shp = (db, s, N_HEADS, MOBA_TOPK, MOBA_BLOCK, HEAD_DIM)
        return ks.reshape(shp), vs.reshape(shp)

    last = page_table[:, n_pages - 1]
    k_loc = jnp.concatenate([cache_k[last].astype(k_new.dtype), k_new], axis=1)
    v_loc = jnp.concatenate([cache_v[last].astype(v_new.dtype), v_new], axis=1)
    loc_pos = past - PAGE_SIZE + jnp.arange(PAGE_SIZE + s)
    q_pos = past + jnp.arange(s)
    return moba_core(q, q_pos, k_means, gather, k_loc, v_loc, loc_pos)


def setup_inputs(seed: int = 0) -> dict:
    key = jax.random.key(seed)
    ks = jax.random.split(key, 20)
    n_pages = PAST_LEN // PAGE_SIZE
    n_used = DEC_BATCH * n_pages
    n_pool = (n_used * 5) // 4
    nrm = jax.random.normal
    x_prompt = nrm(ks[0], (BATCH, SEQ, D_MODEL), jnp.float32)
    x_sample = nrm(ks[1], (DEC_BATCH, DEC_SEQ, D_MODEL), jnp.float32)
    cache_k = nrm(ks[2], (DEPTH, n_pool, PAGE_SIZE, N_HEADS, HEAD_DIM), jnp.float32)
    cache_v = nrm(ks[3], (DEPTH, n_pool, PAGE_SIZE, N_HEADS, HEAD_DIM), jnp.float32)
    state_conv = 0.5 * nrm(ks[4], (DEPTH, DEC_BATCH, CONV_K - 1, CONV_W), jnp.float32)
    perm = jax.random.permutation(ks[5], n_pool)
    page_table = perm[:n_used].reshape(DEC_BATCH, n_pages).astype(jnp.int32)

    def dense(k, shape):
        return nrm(k, shape, jnp.float32) * shape[-2] ** -0.5

    norm_mix = 1.0 + 0.02 * nrm(ks[6], (DEPTH, D_MODEL), jnp.float32)
    w_in = dense(ks[7], (DEPTH, D_MODEL, IN_W))
    q_norm = 1.0 + 0.02 * nrm(ks[8], (DEPTH, HEAD_DIM), jnp.float32)
    k_norm = 1.0 + 0.02 * nrm(ks[9], (DEPTH, HEAD_DIM), jnp.float32)
    conv_w = nrm(ks[10], (DEPTH, CONV_K, CONV_W), jnp.float32) * CONV_K ** -0.5
    w_br_attn = dense(ks[11], (DEPTH, ATTN_W, D_MODEL))
    w_br_conv = dense(ks[12], (DEPTH, CONV_W, D_MODEL))
    w_out = dense(ks[13], (DEPTH, D_MODEL, D_MODEL))
    norm_mlp = 1.0 + 0.02 * nrm(ks[14], (DEPTH, D_MODEL), jnp.float32)
    w_up = dense(ks[15], (DEPTH, D_MODEL, MLP_W))
    w_down = dense(ks[16], (DEPTH, MLP_W, D_MODEL))
    return {'x_prompt': x_prompt, 'x_sample': x_sample, 'cache_k': cache_k, 'cache_v': cache_v,
            'state_conv': state_conv, 'page_table': page_table, 'norm_mix': norm_mix, 'w_in': w_in,
            'q_norm': q_norm, 'k_norm': k_norm, 'conv_w': conv_w, 'w_br_attn': w_br_attn,
            'w_br_conv': w_br_conv, 'w_out': w_out, 'norm_mlp': norm_mlp, 'w_up': w_up, 'w_down': w_down}


def reference(x_prompt, x_sample, cache_k, cache_v, state_conv, page_table, norm_mix, w_in, q_norm, k_norm,
              conv_w, w_br_attn, w_br_conv, w_out, norm_mlp, w_up, w_down):
    b, t = x_prompt.shape[:2]
    s = x_sample.shape[1]
    past = page_table.shape[1] * PAGE_SIZE
    pos_p = jnp.arange(t)
    pos_s = past + jnp.arange(s)
    xp = x_prompt
    xs = x_sample
    kp_l, vp_l, cp_l, ks_l, vs_l, cs_l = [], [], [], [], [], []
    for l in range(DEPTH):
        q, k, v, u, bg, ga, gc = project(xp, norm_mix[l], w_in[l], q_norm[l], k_norm[l], pos_p)
        attn = prompt_attention(q, k, v)
        conv_y, conv_new = short_conv(u, jnp.zeros((b, CONV_K - 1, CONV_W), u.dtype), conv_w[l])
        xp = merge_and_mlp(xp, attn, conv_y, bg, ga, gc, w_br_attn[l], w_br_conv[l], w_out[l],
                           norm_mlp[l], w_up[l], w_down[l])
        kp_l.append(k)
        vp_l.append(v)
        cp_l.append(conv_new)
        q, k, v, u, bg, ga, gc = project(xs, norm_mix[l], w_in[l], q_norm[l], k_norm[l], pos_s)
        attn = sample_attention(q, k, v, cache_k[l], cache_v[l], page_table)
        conv_y, conv_new = short_conv(u, state_conv[l], conv_w[l])
        xs = merge_and_mlp(xs, attn, conv_y, bg, ga, gc, w_br_attn[l], w_br_conv[l], w_out[l],
                           norm_mlp[l], w_up[l], w_down[l])
        ks_l.append(k)
        vs_l.append(v)
        cs_l.append(conv_new)
    return (xp, xs, jnp.stack(kp_l), jnp.stack(vp_l), jnp.stack(cp_l), jnp.stack(ks_l), jnp.stack(vs_l), jnp.stack(cs_l))
```

```python
import functools

import jax
import jax.numpy as jnp
from jax import lax
from jax.experimental import pallas as pl
from jax.experimental.pallas import tpu as pltpu

N_HEADS = 8
HEAD_DIM = 64
ATTN_W = N_HEADS * HEAD_DIM
MOBA_BLOCK = 256
MOBA_TOPK = 3
ROT_DIM = HEAD_DIM // 4
ROT_HALF = ROT_DIM // 2
ROPE_THETA = 500000.0
EPS = 1e-6
NEG = -1e30
CONV_K = 3
SM_SCALE = HEAD_DIM ** -0.5

V7X_VMEM_BYTES = 64 * 1024 * 1024
V7X_SUBLANES = 8
V7X_LANES = 128
MIB = 1024 * 1024

F32 = jnp.float32
BF16 = jnp.bfloat16
HIGHEST = lax.Precision.HIGHEST


def _tiles(n_rows):
    tm = 512 if n_rows % 512 == 0 else n_rows
    return dict(tm=tm, proj_vmem=56 * MIB, attn_vmem=52 * MIB, mlp_vmem=56 * MIB, dec_vmem=40 * MIB)


def _resident(shape):
    nd = len(shape)
    return pl.BlockSpec(shape, lambda *_: (0,) * nd, pipeline_mode=pl.Buffered(1))


def _rms_rows(x, g):
    ms = jnp.mean(x * x, axis=-1, keepdims=True)
    return x * lax.rsqrt(ms + EPS) * g


def _sigmoid(x):
    return 1.0 / (1.0 + jnp.exp(-x))


def _proj_kernel(*refs, tm, pos0, pos_mask, prompt):
    if prompt:
        (x_ref, g_ref, wt_ref, wr_ref, qg_ref, kg_ref, inv_ref, cw_ref,
         qT_ref, kT_ref, vT_ref, y_ref, sga_ref, sgc_ref, u_ref,
         k3_ref, v3_ref, km_ref, carry_ref) = refs
    else:
        (x_ref, g_ref, wt_ref, wr_ref, qg_ref, kg_ref, inv_ref, cw_ref, s1_ref, s2_ref,
         qT_ref, kT_ref, vT_ref, y_ref, sga_ref, sgc_ref, u_ref) = refs
    i = pl.program_id(0)
    width = x_ref.shape[1]

    h = _rms_rows(x_ref[...], g_ref[...]).astype(BF16)

    zT = lax.dot_general(wt_ref[...], h, (((1,), (1,)), ((), ())), preferred_element_type=F32)

    col = lax.broadcasted_iota(jnp.int32, (1, tm), 1) + i * tm
    if pos_mask is not None:
        col = col & pos_mask
    ang = inv_ref[...] * (col + pos0).astype(F32)
    cos = jnp.cos(ang)
    sin = jnp.sin(ang)

    def norm_rope(zt, gain):
        outs = []
        for hh in range(N_HEADS):
            blk = zt[hh * HEAD_DIM:(hh + 1) * HEAD_DIM]
            ms = jnp.mean(blk * blk, axis=0, keepdims=True)
            blk = blk * lax.rsqrt(ms + EPS) * gain
            x1 = blk[0:ROT_HALF]
            x2 = blk[ROT_HALF:ROT_DIM]
            outs += [x1 * cos - x2 * sin, x2 * cos + x1 * sin, blk[ROT_DIM:HEAD_DIM]]
        return jnp.concatenate(outs, axis=0)

    qT = norm_rope(zT[0:ATTN_W], qg_ref[...])
    kT = norm_rope(zT[ATTN_W:2 * ATTN_W], kg_ref[...])
    vT = zT[2 * ATTN_W:3 * ATTN_W]
    qT_ref[...] = qT
    kT_ref[...] = kT
    vT_ref[...] = vT

    if prompt:
        k_rows = kT.T
        for g in range(tm // MOBA_BLOCK):
            kb = k_rows[g * MOBA_BLOCK:(g + 1) * MOBA_BLOCK]
            km_ref[0, g:g + 1, :] = jnp.mean(kb, axis=0, keepdims=True)
            k3_ref[g] = kb.astype(BF16)
            v3_ref[g] = vT[:, g * MOBA_BLOCK:(g + 1) * MOBA_BLOCK].astype(BF16)

    def seg(c):
        return jnp.dot(h, wr_ref[:, c * width:(c + 1) * width], preferred_element_type=F32)

    u = seg(2) * seg(0)
    if prompt:
        @pl.when(i == 0)
        def _():
            carry_ref[...] = jnp.zeros_like(carry_ref)
        carry = carry_ref[...]
    else:
        carry = jnp.zeros((V7X_SUBLANES, width), F32)
    up = jnp.concatenate([carry, u], axis=0)
    u1 = up[V7X_SUBLANES - 1:V7X_SUBLANES - 1 + tm]
    u2 = up[V7X_SUBLANES - 2:V7X_SUBLANES - 2 + tm]
    if not prompt:
        rm = lax.broadcasted_iota(jnp.int32, (tm, 1), 0) & pos_mask
        u1 = jnp.where(rm >= 1, u1, s1_ref[...])
        u2 = jnp.where(rm >= 2, u2, s2_ref[...])
    cw = cw_ref[...]
    conv = cw[0:1] * u2 + cw[1:2] * u1 + cw[2:3] * u
    y_ref[...] = (seg(1) * conv).astype(BF16)
    sga_ref[...] = _sigmoid(seg(3)).astype(BF16)
    sgc_ref[...] = _sigmoid(seg(4)).astype(BF16)
    if prompt:
        tail = u[tm - V7X_SUBLANES:tm]
        carry_ref[...] = tail
        u_ref[...] = tail
    else:
        u_ref[...] = u


def _project(x2d, g_mix, w_qkvT, w_rest, qg, kg, inv, cw, *, pos0, prev=None):
    n, d = x2d.shape
    cfg = _tiles(n)
    tm = cfg["tm"]
    prompt = prev is None
    nt = n // tm
    row = lambda w: pl.BlockSpec((tm, w), lambda i: (i, 0))
    colT = pl.BlockSpec((ATTN_W, tm), lambda i: (0, i))
    in_specs = [row(d), _resident(g_mix.shape), _resident(w_qkvT.shape), _resident(w_rest.shape),
                _resident(qg.shape), _resident(kg.shape), _resident(inv.shape), _resident(cw.shape)]
    args = [x2d, g_mix, w_qkvT, w_rest, qg, kg, inv, cw]
    out_shape = [jax.ShapeDtypeStruct((ATTN_W, n), F32)] * 3 + [jax.ShapeDtypeStruct((n, d), BF16)] * 3
    out_specs = [colT, colT, colT, row(d), row(d), row(d)]
    scratch = []
    if prompt:
        assert tm % MOBA_BLOCK == 0
        gpt = tm // MOBA_BLOCK
        out_shape += [jax.ShapeDtypeStruct((V7X_SUBLANES, d), F32),
                      jax.ShapeDtypeStruct((n // MOBA_BLOCK, MOBA_BLOCK, ATTN_W), BF16),
                      jax.ShapeDtypeStruct((n // MOBA_BLOCK, ATTN_W, MOBA_BLOCK), BF16),
                      jax.ShapeDtypeStruct((nt, gpt, ATTN_W), F32)]
        out_specs += [pl.BlockSpec((V7X_SUBLANES, d), lambda i: (0, 0)),
                      pl.BlockSpec((gpt, MOBA_BLOCK, ATTN_W), lambda i: (i, 0, 0)),
                      pl.BlockSpec((gpt, ATTN_W, MOBA_BLOCK), lambda i: (i, 0, 0)),
                      pl.BlockSpec((1, gpt, ATTN_W), lambda i: (i, 0, 0))]
        scratch = [pltpu.VMEM((V7X_SUBLANES, d), F32)]
        pos_mask = None
    else:
        s1, s2, seq = prev
        assert seq & (seq - 1) == 0
        pos_mask = seq - 1
        in_specs += [row(d), row(d)]
        args += [s1, s2]
        out_shape += [jax.ShapeDtypeStruct((n, d), F32)]
        out_specs += [row(d)]
    return pl.pallas_call(
        functools.partial(_proj_kernel, tm=tm, pos0=pos0, pos_mask=pos_mask, prompt=prompt),
        grid=(nt,), in_specs=in_specs, out_specs=out_specs, out_shape=out_shape, scratch_shapes=scratch,
        compiler_params=pltpu.CompilerParams(dimension_semantics=("arbitrary",),
                                             vmem_limit_bytes=cfg["proj_vmem"]),
        name="proj_prompt" if prompt else "proj_sample",
    )(*args)


def _attn_kernel(q_ref, km_ref, k_hbm, v_hbm, o_ref,
                 kbuf, vbuf, qm_ref, sel_ref, acc_ref, m_ref, l_ref, sem, *, nb):
    i = pl.program_id(0)
    blk_q = MOBA_BLOCK

    @pl.when(i == 0)
    def _():
        ck = pltpu.make_async_copy(k_hbm, kbuf, sem.at[0])
        cv = pltpu.make_async_copy(v_hbm, vbuf, sem.at[1])
        ck.start()
        cv.start()
        ck.wait()
        cv.wait()

    pair_w = 2 * HEAD_DIM
    blk = lax.broadcasted_iota(jnp.int32, (nb, blk_q), 0)
    rid = lax.broadcasted_iota(jnp.int32, (pair_w, 1), 0)
    for h in range(N_HEADS):
        pr = h // 2
        qp = q_ref[pr * pair_w:(pr + 1) * pair_w, :]
        keep = (rid >= HEAD_DIM) if h % 2 else (rid < HEAD_DIM)
        qm = jnp.where(keep, qp, 0.0)
        gate = jnp.dot(km_ref[:, pr * pair_w:(pr + 1) * pair_w], qm,
                       precision=HIGHEST, preferred_element_type=F32)
        g = jnp.where(blk < i, gate, -jnp.inf)
        sel = jnp.zeros((nb, blk_q), F32)
        for r in range(MOBA_TOPK):
            mx = jnp.max(g, axis=0, keepdims=True)
            idx = jnp.min(jnp.where(g == mx, blk, nb), axis=0, keepdims=True)
            hit = blk == idx
            sel = jnp.where(jnp.logical_and(hit, i > r), 1.0, sel)
            g = jnp.where(hit, -jnp.inf, g)
        sel_ref[h] = sel
        qm_ref[h] = (qm * SM_SCALE).astype(BF16)

    kq = lax.broadcasted_iota(jnp.int32, (blk_q, blk_q), 0)
    qq = lax.broadcasted_iota(jnp.int32, (blk_q, blk_q), 1)
    causal = kq <= qq
    for h in range(N_HEADS):
        pr = h // 2
        hs = slice(h * HEAD_DIM, (h + 1) * HEAD_DIM)
        s = jnp.dot(kbuf[i, :, pr * pair_w:(pr + 1) * pair_w], qm_ref[h], preferred_element_type=F32)
        s = jnp.where(causal, s, NEG)
        m = jnp.max(s, axis=0, keepdims=True)
        p = jnp.exp(s - m)
        m_ref[h:h + 1, :] = m
        l_ref[h:h + 1, :] = jnp.sum(p, axis=0, keepdims=True)
        acc_ref[hs, :] = jnp.dot(vbuf[i, hs, :], p.astype(BF16), preferred_element_type=F32)

    def body(j, c):
        for h in range(N_HEADS):
            pr = h // 2
            hs = slice(h * HEAD_DIM, (h + 1) * HEAD_DIM)
            s = jnp.dot(kbuf[j, :, pr * pair_w:(pr + 1) * pair_w], qm_ref[h], preferred_element_type=F32)
            s = jnp.where(sel_ref[h, pl.ds(j, 1), :] > 0.0, s, NEG)
            m_old = m_ref[h:h + 1, :]
            m_new = jnp.maximum(m_old, jnp.max(s, axis=0, keepdims=True))
            alpha = jnp.exp(m_old - m_new)
            p = jnp.exp(s - m_new)
            m_ref[h:h + 1, :] = m_new
            l_ref[h:h + 1, :] = alpha * l_ref[h:h + 1, :] + jnp.sum(p, axis=0, keepdims=True)
            acc_ref[hs, :] = alpha * acc_ref[hs, :] + jnp.dot(vbuf[j, hs, :], p.astype(BF16),
                                                              preferred_element_type=F32)
        return c

    lax.fori_loop(0, i, body, 0)

    for h in range(N_HEADS):
        hs = slice(h * HEAD_DIM, (h + 1) * HEAD_DIM)
        acc_ref[hs, :] = acc_ref[hs, :] / l_ref[h:h + 1, :]
    o_ref[...] = acc_ref[...].T.astype(BF16)


def _prompt_attention(qT, kmeans, k3, v3):
    nb = k3.shape[0]
    t = nb * MOBA_BLOCK
    cfg = _tiles(t)
    return pl.pallas_call(
        functools.partial(_attn_kernel, nb=nb),
        grid=(nb,),
        in_specs=[pl.BlockSpec((ATTN_W, MOBA_BLOCK), lambda i: (0, i)),
                  _resident(kmeans.shape),
                  pl.BlockSpec(memory_space=pl.ANY),
                  pl.BlockSpec(memory_space=pl.ANY)],
        out_specs=pl.BlockSpec((MOBA_BLOCK, ATTN_W), lambda i: (i, 0)),
        out_shape=jax.ShapeDtypeStruct((t, ATTN_W), BF16),
        scratch_shapes=[pltpu.VMEM(k3.shape, BF16), pltpu.VMEM(v3.shape, BF16),
                        pltpu.VMEM((N_HEADS, 2 * HEAD_DIM, MOBA_BLOCK), BF16),
                        pltpu.VMEM((N_HEADS, nb, MOBA_BLOCK), F32),
                        pltpu.VMEM((ATTN_W, MOBA_BLOCK), F32),
                        pltpu.VMEM((N_HEADS, MOBA_BLOCK), F32),
                        pltpu.VMEM((N_HEADS, MOBA_BLOCK), F32),
                        pltpu.SemaphoreType.DMA((2,))],
        compiler_params=pltpu.CompilerParams(dimension_semantics=("arbitrary",),
                                             vmem_limit_bytes=cfg["attn_vmem"]),
        name="moba_prompt",
    )(qT, kmeans, k3, v3)


def _dec_kernel(pt_ref, qs_ref, kn_ref, vn_ref, ck_hbm, cv_hbm, o_ref,
                kring, kmt_ref, idxv_ref, idxs_ref, kg, vg, vlast, sem_ring, sem_g, sem_misc,
                *, n_pages, nbuf, seq):
    b = pl.program_id(0)
    ppb = MOBA_BLOCK // V7X_LANES
    nblk = n_pages // ppb
    n_sel = seq * MOBA_TOPK * ppb

    def page_copy(p, slot):
        return pltpu.make_async_copy(ck_hbm.at[pt_ref[b, p]], kring.at[slot], sem_ring.at[slot])

    for p in range(nbuf):
        page_copy(p, p).start()

    lane = lax.broadcasted_iota(jnp.int32, (ATTN_W, V7X_LANES), 1)
    kmt_ref[...] = jnp.zeros_like(kmt_ref)

    def blk_body(j, c):
        bs = None
        for pg in range(ppb):
            p = j * ppb + pg
            slot = p % nbuf
            page_copy(p, slot).wait()
            bs = kring[slot] if bs is None else bs + kring[slot]
        colsum = jnp.sum(bs, axis=1, keepdims=True)
        kmt_ref[...] = jnp.where(lane == j, colsum, kmt_ref[...])
        for pg in range(ppb):
            p = j * ppb + pg

            @pl.when(p + nbuf < n_pages)
            def _():
                page_copy(p + nbuf, p % nbuf).start()
        return c

    lax.fori_loop(0, nblk, blk_body, 0)

    kmt = kmt_ref[...] * (1.0 / MOBA_BLOCK)
    sub8 = lax.broadcasted_iota(jnp.int32, (N_HEADS, V7X_LANES), 0)
    lane8 = lax.broadcasted_iota(jnp.int32, (N_HEADS, V7X_LANES), 1)
    idxv = jnp.zeros((N_HEADS, V7X_LANES), jnp.int32)
    for s in range(seq):
        qcol = qs_ref[0, :, s:s + 1]
        g = jnp.full((N_HEADS, V7X_LANES), -jnp.inf, F32)
        for h in range(N_HEADS):
            hs = slice(h * HEAD_DIM, (h + 1) * HEAD_DIM)
            gh = jnp.sum(kmt[hs] * qcol[hs], axis=0, keepdims=True)
            g = jnp.where(jnp.logical_and(sub8 == h, lane8 < nblk), gh, g)
        for r in range(MOBA_TOPK):
            mx = jnp.max(g, axis=1, keepdims=True)
            idx = jnp.min(jnp.where(g == mx, lane8, V7X_LANES), axis=1, keepdims=True)
            idxv = jnp.where(lane8 == s * MOBA_TOPK + r, idx, idxv)
            g = jnp.where(lane8 == idx, -jnp.inf, g)
    idxv_ref[...] = idxv
    cp = pltpu.make_async_copy(idxv_ref, idxs_ref, sem_misc.at[0])
    cp.start()
    cp.wait()

    def tile_copies(n):
        h = n % N_HEADS
        t = n // N_HEADS
        pg = t % ppb
        sr = t // ppb
        page = pt_ref[b, idxs_ref[h, sr] * ppb + pg]
        rows = pl.ds(pl.multiple_of(h * HEAD_DIM, HEAD_DIM), HEAD_DIM)
        return (pltpu.make_async_copy(ck_hbm.at[page, rows], kg.at[t, rows], sem_g.at[0]),
                pltpu.make_async_copy(cv_hbm.at[page, rows], vg.at[t, rows], sem_g.at[1]))

    def issue(n, c):
        ck, cv = tile_copies(n)
        ck.start()
        cv.start()
        return c

    lax.fori_loop(0, n_sel * N_HEADS, issue, 0)
    last = pltpu.make_async_copy(cv_hbm.at[pt_ref[b, n_pages - 1]], vlast, sem_misc.at[1])
    last.start()

    def drain(n, c):
        ck, cv = tile_copies(n)
        ck.wait()
        cv.wait()
        return c

    lax.fori_loop(0, n_sel * N_HEADS, drain, 0)
    last.wait()

    lane_s = lax.broadcasted_iota(jnp.int32, (1, seq), 1)
    per_q = MOBA_TOPK * ppb
    cols = []
    for s in range(seq):
        qcol = qs_ref[0, :, s:s + 1] * SM_SCALE
        outs = []
        for h in range(N_HEADS):
            hs = slice(h * HEAD_DIM, (h + 1) * HEAD_DIM)
            qh = qcol[hs]
            scs = [jnp.sum(kg[s * per_q + t, hs, :] * qh, axis=0, keepdims=True) for t in range(per_q)]
            loc = jnp.sum(kn_ref[0, hs, :] * qh, axis=0, keepdims=True)
            loc = jnp.where(lane_s <= s, loc, NEG)
            mt = scs[0]
            for sc in scs[1:]:
                mt = jnp.maximum(mt, sc)
            m = jnp.maximum(jnp.max(mt, axis=1, keepdims=True), jnp.max(loc, axis=1, keepdims=True))
            ps = [jnp.exp(sc - m) for sc in scs]
            ploc = jnp.exp(loc - m)
            plast = jnp.exp(NEG - m)
            psum = ps[0]
            for p_ in ps[1:]:
                psum = psum + p_
            l = (jnp.sum(psum, axis=1, keepdims=True) + jnp.sum(ploc, axis=1, keepdims=True)
                 + V7X_LANES * plast)
            acc = vlast[hs, :] * plast
            for t in range(per_q):
                acc = acc + vg[s * per_q + t, hs, :] * ps[t]
            o = jnp.sum(acc, axis=1, keepdims=True)
            o = o + jnp.sum(vn_ref[0, hs, :] * ploc, axis=1, keepdims=True)
            outs.append(o / l)
        cols.append(jnp.concatenate(outs, axis=0))
    o_ref[0] = jnp.concatenate(cols, axis=1)


def _decode_attention(page_table, qs, kn, vn, ckT, cvT):
    bsz, _, seq = qs.shape
    n_pages = page_table.shape[1]
    nbuf = min(16, n_pages)
    ppb = MOBA_BLOCK // V7X_LANES
    assert ckT.shape[1:] == (ATTN_W, V7X_LANES) and n_pages % ppb == 0 and nbuf % ppb == 0
    assert n_pages // ppb <= V7X_LANES and seq * MOBA_TOPK <= V7X_LANES
    n_tiles = seq * MOBA_TOPK * ppb
    cfg = _tiles(bsz * seq)
    tok = pl.BlockSpec((1, ATTN_W, seq), lambda b, pt: (b, 0, 0))
    grid_spec = pltpu.PrefetchScalarGridSpec(
        num_scalar_prefetch=1, grid=(bsz,),
        in_specs=[tok, tok, tok, pl.BlockSpec(memory_space=pl.ANY), pl.BlockSpec(memory_space=pl.ANY)],
        out_specs=tok,
        scratch_shapes=[pltpu.VMEM((nbuf, ATTN_W, V7X_LANES), F32),
                        pltpu.VMEM((ATTN_W, V7X_LANES), F32),
                        pltpu.VMEM((N_HEADS, V7X_LANES), jnp.int32),
                        pltpu.SMEM((N_HEADS, V7X_LANES), jnp.int32),
                        pltpu.VMEM((n_tiles, ATTN_W, V7X_LANES), F32),
                        pltpu.VMEM((n_tiles, ATTN_W, V7X_LANES), F32),
                        pltpu.VMEM((ATTN_W, V7X_LANES), F32),
                        pltpu.SemaphoreType.DMA((nbuf,)),
                        pltpu.SemaphoreType.DMA((2,)),
                        pltpu.SemaphoreType.DMA((2,))])
    return pl.pallas_call(
        functools.partial(_dec_kernel, n_pages=n_pages, nbuf=nbuf, seq=seq),
        grid_spec=grid_spec,
        out_shape=jax.ShapeDtypeStruct((bsz, ATTN_W, seq), F32),
        compiler_params=pltpu.CompilerParams(dimension_semantics=("arbitrary",),
                                             vmem_limit_bytes=cfg["dec_vmem"]),
        name="moba_decode",
    )(page_table, qs, kn, vn, ckT, cvT)


def _mlp_kernel(x_ref, a_ref, y_ref, sga_ref, sgc_ref, wba_ref, wbc_ref, wo_ref, g_ref, wup_ref, wdn_ref,
                o_ref, *, chunk):
    a = jnp.dot(a_ref[...], wba_ref[...], preferred_element_type=F32)
    c = jnp.dot(y_ref[...], wbc_ref[...], preferred_element_type=F32)
    mixed = sga_ref[...].astype(F32) * a + sgc_ref[...].astype(F32) * c
    x1 = x_ref[...] + jnp.dot(mixed.astype(BF16), wo_ref[...], preferred_element_type=F32)
    h2 = _rms_rows(x1, g_ref[...]).astype(BF16)
    acc = x1
    for c0 in range(0, wup_ref.shape[1], chunk):
        hid = jnp.maximum(jnp.dot(h2, wup_ref[:, c0:c0 + chunk], preferred_element_type=F32), 0.0)
        acc = acc + jnp.dot((hid * hid).astype(BF16), wdn_ref[c0:c0 + chunk, :], preferred_element_type=F32)
    o_ref[...] = acc


def _merge_mlp(x2d, attn, y, sga, sgc, wba, wbc, wo, g_mlp, wup, wdn):
    n, d = x2d.shape
    cfg = _tiles(n)
    tm = cfg["tm"]
    row = lambda w: pl.BlockSpec((tm, w), lambda i: (i, 0))
    weights = [wba, wbc, wo, g_mlp, wup, wdn]
    return pl.pallas_call(
        functools.partial(_mlp_kernel, chunk=d),
        grid=(n // tm,),
        in_specs=[row(d), row(attn.shape[1]), row(d), row(d), row(d)] + [_resident(w.shape) for w in weights],
        out_specs=row(d),
        out_shape=jax.ShapeDtypeStruct((n, d), F32),
        compiler_params=pltpu.CompilerParams(dimension_semantics=("arbitrary",),
                                             vmem_limit_bytes=cfg["mlp_vmem"]),
        name="merge_mlp",
    )(x2d, attn, y, sga, sgc, *weights)


def _heads_last(xT, lead):
    n = xT.shape[1]
    return jnp.transpose(xT.reshape(N_HEADS, HEAD_DIM, n), (2, 0, 1)).reshape(*lead, N_HEADS, HEAD_DIM)


def kernel(x_prompt, x_sample, cache_k, cache_v, state_conv, page_table, norm_mix, w_in, q_norm, k_norm, conv_w, w_br_attn, w_br_conv, w_out, norm_mlp, w_up, w_down):
    bp, t, d = x_prompt.shape
    db, s, _ = x_sample.shape
    depth = w_in.shape[0]
    n_pool, page = cache_k.shape[1], cache_k.shape[2]
    assert bp == 1 and page == V7X_LANES and t % MOBA_BLOCK == 0
    past = page_table.shape[1] * page
    inv = (ROPE_THETA ** (-jnp.arange(ROT_HALF, dtype=F32) * 2.0 / ROT_DIM)).reshape(ROT_HALF, 1)

    xp = x_prompt.reshape(t, d)
    xs = x_sample.reshape(db * s, d)
    kp_l, vp_l, cp_l, ks_l, vs_l, cs_l = [], [], [], [], [], []
    for l in range(depth):
        w_qkvT = w_in[l][:, :3 * ATTN_W].T.astype(BF16)
        w_rest = w_in[l][:, 3 * ATTN_W:].astype(BF16)
        g_mix = norm_mix[l].reshape(1, d)
        qg = q_norm[l].reshape(HEAD_DIM, 1)
        kg = k_norm[l].reshape(HEAD_DIM, 1)
        cw = conv_w[l]
        weights = (w_br_attn[l].astype(BF16), w_br_conv[l].astype(BF16), w_out[l].astype(BF16),
                   norm_mlp[l].reshape(1, d), w_up[l].astype(BF16), w_down[l].astype(BF16))

        qT, kT, vT, y, sga, sgc, u_tail, k3, v3, km = _project(xp, g_mix, w_qkvT, w_rest, qg, kg, inv, cw, pos0=0)
        attn = _prompt_attention(qT, km.reshape(t // MOBA_BLOCK, ATTN_W), k3, v3)
        xp = _merge_mlp(xp, attn, y, sga, sgc, *weights)
        kp_l.append(_heads_last(kT, (bp, t)))
        vp_l.append(_heads_last(vT, (bp, t)))
        cp_l.append(u_tail[V7X_SUBLANES - (CONV_K - 1):].reshape(bp, CONV_K - 1, d))

        st = state_conv[l]
        zero = jnp.zeros((db, s, d), F32)
        s1 = zero.at[:, 0].set(st[:, 1]).reshape(db * s, d)
        s2 = zero.at[:, 0].set(st[:, 0]).at[:, 1].set(st[:, 1]).reshape(db * s, d)
        qT, kT, vT, y, sga, sgc, u = _project(xs, g_mix, w_qkvT, w_rest, qg, kg, inv, cw, pos0=past,
                                               prev=(s1, s2, s))
        per_seq = lambda xT: jnp.transpose(xT.reshape(ATTN_W, db, s), (1, 0, 2))
        ckT = jnp.transpose(cache_k[l], (0, 2, 3, 1)).reshape(n_pool, ATTN_W, page)
        cvT = jnp.transpose(cache_v[l], (0, 2, 3, 1)).reshape(n_pool, ATTN_W, page)
        oT = _decode_attention(page_table, per_seq(qT), per_seq(kT), per_seq(vT), ckT, cvT)
        attn = jnp.transpose(oT, (0, 2, 1)).reshape(db * s, ATTN_W).astype(BF16)
        xs = _merge_mlp(xs, attn, y, sga, sgc, *weights)
        ks_l.append(_heads_last(kT, (db, s)))
        vs_l.append(_heads_last(vT, (db, s)))
        cs_l.append(u.reshape(db, s, d)[:, s - (CONV_K - 1):])

    return (xp.reshape(bp, t, d), xs.reshape(db, s, d), jnp.stack(kp_l), jnp.stack(vp_l), jnp.stack(cp_l),
            jnp.stack(ks_l), jnp.stack(vs_l), jnp.stack(cs_l))
```

```python
import functools

import jax
import jax.numpy as jnp
from jax import lax
from jax.experimental import pallas as pl
from jax.experimental.pallas import tpu as pltpu

N_HEADS = 8
HEAD_DIM = 64
ATTN_W = N_HEADS * HEAD_DIM
MOBA_BLOCK = 256
MOBA_TOPK = 3
ROT_DIM = HEAD_DIM // 4
ROT_HALF = ROT_DIM // 2
ROPE_THETA = 500000.0
EPS = 1e-6
NEG = -1e30
CONV_K = 3
SM_SCALE = HEAD_DIM ** -0.5
LOG2E = 1.4426950408889634
BIG = 1e30
V_ROWS = HEAD_DIM + 16

V7X_VMEM_BYTES = 64 * 1024 * 1024
V7X_SUBLANES = 8
V7X_LANES = 128
MIB = 1024 * 1024

F32 = jnp.float32
BF16 = jnp.bfloat16
HIGHEST = lax.Precision.HIGHEST


def _tiles(n_rows):
    tm = 512 if n_rows % 512 == 0 else n_rows
    return dict(tm=tm, proj_vmem=56 * MIB, attn_vmem=52 * MIB, mlp_vmem=56 * MIB, dec_vmem=40 * MIB,
                attn_q_split=1, attn_look=8, attn_unroll=4)


def _resident(shape):
    nd = len(shape)
    return pl.BlockSpec(shape, lambda *_: (0,) * nd, pipeline_mode=pl.Buffered(1))


def _rms_rows(x, g):
    ms = jnp.mean(x * x, axis=-1, keepdims=True)
    return x * lax.rsqrt(ms + EPS) * g


def _sigmoid(x):
    return 1.0 / (1.0 + jnp.exp(-x))


def _proj_kernel(*refs, tm, pos0, pos_mask, prompt):
    if prompt:
        (x_ref, g_ref, wt_ref, wr_ref, qg_ref, kg_ref, inv_ref, cw_ref,
         qT_ref, kT_ref, vT_ref, y_ref, sga_ref, sgc_ref, u_ref,
         k3_ref, v3_ref, km_ref, carry_ref) = refs
    else:
        (x_ref, g_ref, wt_ref, wr_ref, qg_ref, kg_ref, inv_ref, cw_ref, s1_ref, s2_ref,
         qT_ref, kT_ref, vT_ref, y_ref, sga_ref, sgc_ref, u_ref) = refs
    i = pl.program_id(0)
    width = x_ref.shape[1]

    h = _rms_rows(x_ref[...], g_ref[...]).astype(BF16)

    zT = lax.dot_general(wt_ref[...], h, (((1,), (1,)), ((), ())), preferred_element_type=F32)

    col = lax.broadcasted_iota(jnp.int32, (1, tm), 1) + i * tm
    if pos_mask is not None:
        col = col & pos_mask
    ang = inv_ref[...] * (col + pos0).astype(F32)
    cos = jnp.cos(ang)
    sin = jnp.sin(ang)

    def norm_rope(zt, gain):
        outs = []
        for hh in range(N_HEADS):
            blk = zt[hh * HEAD_DIM:(hh + 1) * HEAD_DIM]
            ms = jnp.mean(blk * blk, axis=0, keepdims=True)
            blk = blk * lax.rsqrt(ms + EPS) * gain
            x1 = blk[0:ROT_HALF]
            x2 = blk[ROT_HALF:ROT_DIM]
            outs += [x1 * cos - x2 * sin, x2 * cos + x1 * sin, blk[ROT_DIM:HEAD_DIM]]
        return jnp.concatenate(outs, axis=0)

    qT = norm_rope(zT[0:ATTN_W], qg_ref[...])
    kT = norm_rope(zT[ATTN_W:2 * ATTN_W], kg_ref[...])
    vT = zT[2 * ATTN_W:3 * ATTN_W]
    qT_ref[...] = qT
    kT_ref[...] = kT
    vT_ref[...] = vT

    if prompt:
        k_rows = kT.T
        pad_row = lax.broadcasted_iota(jnp.int32, (V_ROWS - HEAD_DIM, MOBA_BLOCK), 0)
        ones_pad = jnp.where(pad_row == 0, 1.0, 0.0).astype(BF16)
        for g in range(tm // MOBA_BLOCK):
            kb = k_rows[g * MOBA_BLOCK:(g + 1) * MOBA_BLOCK]
            km_ref[0, g:g + 1, :] = jnp.mean(kb, axis=0, keepdims=True)
            k3_ref[g] = kb.astype(BF16)
            vb = vT[:, g * MOBA_BLOCK:(g + 1) * MOBA_BLOCK].astype(BF16)
            pieces = []
            for hh in range(N_HEADS):
                pieces += [vb[hh * HEAD_DIM:(hh + 1) * HEAD_DIM], ones_pad]
            v3_ref[g] = jnp.concatenate(pieces, axis=0)

    def seg(c):
        return jnp.dot(h, wr_ref[:, c * width:(c + 1) * width], preferred_element_type=F32)

    u = seg(2) * seg(0)
    if prompt:
        @pl.when(i == 0)
        def _():
            carry_ref[...] = jnp.zeros_like(carry_ref)
        carry = carry_ref[...]
    else:
        carry = jnp.zeros((V7X_SUBLANES, width), F32)
    up = jnp.concatenate([carry, u], axis=0)
    u1 = up[V7X_SUBLANES - 1:V7X_SUBLANES - 1 + tm]
    u2 = up[V7X_SUBLANES - 2:V7X_SUBLANES - 2 + tm]
    if not prompt:
        rm = lax.broadcasted_iota(jnp.int32, (tm, 1), 0) & pos_mask
        u1 = jnp.where(rm >= 1, u1, s1_ref[...])
        u2 = jnp.where(rm >= 2, u2, s2_ref[...])
    cw = cw_ref[...]
    conv = cw[0:1] * u2 + cw[1:2] * u1 + cw[2:3] * u
    y_ref[...] = (seg(1) * conv).astype(BF16)
    sga_ref[...] = _sigmoid(seg(3)).astype(BF16)
    sgc_ref[...] = _sigmoid(seg(4)).astype(BF16)
    if prompt:
        tail = u[tm - V7X_SUBLANES:tm]
        carry_ref[...] = tail
        u_ref[...] = tail
    else:
        u_ref[...] = u


def _project(x2d, g_mix, w_qkvT, w_rest, qg, kg, inv, cw, *, pos0, prev=None):
    n, d = x2d.shape
    cfg = _tiles(n)
    tm = cfg["tm"]
    prompt = prev is None
    nt = n // tm
    row = lambda w: pl.BlockSpec((tm, w), lambda i: (i, 0))
    colT = pl.BlockSpec((ATTN_W, tm), lambda i: (0, i))
    in_specs = [row(d), _resident(g_mix.shape), _resident(w_qkvT.shape), _resident(w_rest.shape),
                _resident(qg.shape), _resident(kg.shape), _resident(inv.shape), _resident(cw.shape)]
    args = [x2d, g_mix, w_qkvT, w_rest, qg, kg, inv, cw]
    out_shape = [jax.ShapeDtypeStruct((ATTN_W, n), F32)] * 3 + [jax.ShapeDtypeStruct((n, d), BF16)] * 3
    out_specs = [colT, colT, colT, row(d), row(d), row(d)]
    scratch = []
    if prompt:
        assert tm % MOBA_BLOCK == 0
        gpt = tm // MOBA_BLOCK
        out_shape += [jax.ShapeDtypeStruct((V7X_SUBLANES, d), F32),
                      jax.ShapeDtypeStruct((n // MOBA_BLOCK, MOBA_BLOCK, ATTN_W), BF16),
                      jax.ShapeDtypeStruct((n // MOBA_BLOCK, N_HEADS * V_ROWS, MOBA_BLOCK), BF16),
                      jax.ShapeDtypeStruct((nt, gpt, ATTN_W), F32)]
        out_specs += [pl.BlockSpec((V7X_SUBLANES, d), lambda i: (0, 0)),
                      pl.BlockSpec((gpt, MOBA_BLOCK, ATTN_W), lambda i: (i, 0, 0)),
                      pl.BlockSpec((gpt, N_HEADS * V_ROWS, MOBA_BLOCK), lambda i: (i, 0, 0)),
                      pl.BlockSpec((1, gpt, ATTN_W), lambda i: (i, 0, 0))]
        scratch = [pltpu.VMEM((V7X_SUBLANES, d), F32)]
        pos_mask = None
    else:
        s1, s2, seq = prev
        assert seq & (seq - 1) == 0
        pos_mask = seq - 1
        in_specs += [row(d), row(d)]
        args += [s1, s2]
        out_shape += [jax.ShapeDtypeStruct((n, d), F32)]
        out_specs += [row(d)]
    return pl.pallas_call(
        functools.partial(_proj_kernel, tm=tm, pos0=pos0, pos_mask=pos_mask, prompt=prompt),
        grid=(nt,), in_specs=in_specs, out_specs=out_specs, out_shape=out_shape, scratch_shapes=scratch,
        compiler_params=pltpu.CompilerParams(dimension_semantics=("arbitrary",),
                                             vmem_limit_bytes=cfg["proj_vmem"]),
        name="proj_prompt" if prompt else "proj_sample",
    )(*args)


def _attn_kernel(q_ref, km_ref, k_hbm, v_hbm, o_ref,
                 kbuf, vbuf, qm_ref, sel_ref, acc_ref, m_ref, sem, *, nb, q_split, look, unroll):
    i = pl.program_id(0)
    blk_q = MOBA_BLOCK

    @pl.when(i == 0)
    def _():
        ck = pltpu.make_async_copy(k_hbm, kbuf, sem.at[0])
        cv = pltpu.make_async_copy(v_hbm, vbuf, sem.at[1])
        ck.start()
        cv.start()
        ck.wait()
        cv.wait()

    pair_w = 2 * HEAD_DIM
    blk = lax.broadcasted_iota(jnp.int32, (nb, blk_q), 0)
    rid = lax.broadcasted_iota(jnp.int32, (pair_w, 1), 0)
    for h in range(N_HEADS):
        pr = h // 2
        qp = q_ref[pr * pair_w:(pr + 1) * pair_w, :]
        keep = (rid >= HEAD_DIM) if h % 2 else (rid < HEAD_DIM)
        qm = jnp.where(keep, qp, 0.0)
        gate = jnp.dot(km_ref[:, pr * pair_w:(pr + 1) * pair_w], qm,
                       precision=HIGHEST, preferred_element_type=F32)
        g = jnp.where(blk < i, gate, -jnp.inf)
        sel = jnp.zeros((nb, blk_q), F32)
        for r in range(MOBA_TOPK):
            mx = jnp.max(g, axis=0, keepdims=True)
            idx = jnp.min(jnp.where(g == mx, blk, nb), axis=0, keepdims=True)
            hit = blk == idx
            sel = jnp.where(jnp.logical_and(hit, i > r), 1.0, sel)
            g = jnp.where(hit, -jnp.inf, g)
        sel_ref[h] = sel
        qm_ref[h] = (qm * (SM_SCALE * LOG2E)).astype(BF16)

    qw = blk_q // q_split

    def scores(j, h, c):
        pr = h // 2
        return jnp.dot(kbuf[j, :, pr * pair_w:(pr + 1) * pair_w], qm_ref[h, :, c * qw:(c + 1) * qw],
                       preferred_element_type=F32)

    def pv(j, h, p):
        return jnp.dot(vbuf[j, h * V_ROWS:(h + 1) * V_ROWS, :], p.astype(BF16), preferred_element_type=F32)

    kq = lax.broadcasted_iota(jnp.int32, (blk_q, qw), 0)
    qq = lax.broadcasted_iota(jnp.int32, (blk_q, qw), 1)

    def run_tiles(js, own):
        tiles = [(j, h, c) for j in js for h in range(N_HEADS) for c in range(q_split)]
        pending = {}
        for k in range(len(tiles) + look):
            if k < len(tiles):
                pending[k] = scores(*tiles[k])
            if k < look:
                continue
            j, h, c = tiles[k - look]
            s = pending.pop(k - look)
            cols = slice(c * qw, (c + 1) * qw)
            rows = slice(h * V_ROWS, (h + 1) * V_ROWS)
            if own:
                s = jnp.where(kq <= qq + c * qw, s, NEG)
                m = jnp.max(s, axis=0, keepdims=True)
                m_ref[h:h + 1, cols] = m
                acc_ref[rows, cols] = pv(j, h, jnp.exp2(s - m))
            else:
                picked = sel_ref[h, pl.ds(j, 1), :][:, cols] > 0.0
                m_old = m_ref[h:h + 1, cols]
                m_new = jnp.where(picked, jnp.maximum(m_old, jnp.max(s, axis=0, keepdims=True)), m_old)
                m_ref[h:h + 1, cols] = m_new
                part = pv(j, h, jnp.exp2(s - jnp.where(picked, m_new, BIG)))
                acc_ref[rows, cols] = jnp.exp2(m_old - m_new) * acc_ref[rows, cols] + part

    run_tiles([i], own=True)

    def past_body(jj, c):
        run_tiles([unroll * jj + b for b in range(unroll)], own=False)
        return c

    lax.fori_loop(0, i // unroll, past_body, 0)
    if unroll > 1:
        def tail_body(j, c):
            run_tiles([j], own=False)
            return c

        lax.fori_loop((i // unroll) * unroll, i, tail_body, 0)

    outs = []
    for h in range(N_HEADS):
        a = acc_ref[h * V_ROWS:(h + 1) * V_ROWS, :]
        outs.append(a[0:HEAD_DIM] / a[HEAD_DIM:HEAD_DIM + 1])
    o_ref[...] = jnp.concatenate(outs, axis=0).T.astype(BF16)


def _prompt_attention(qT, kmeans, k3, v3):
    nb = k3.shape[0]
    t = nb * MOBA_BLOCK
    cfg = _tiles(t)
    return pl.pallas_call(
        functools.partial(_attn_kernel, nb=nb, q_split=cfg["attn_q_split"], look=cfg["attn_look"],
                          unroll=cfg["attn_unroll"]),
        grid=(nb,),
        in_specs=[pl.BlockSpec((ATTN_W, MOBA_BLOCK), lambda i: (0, i)),
                  _resident(kmeans.shape),
                  pl.BlockSpec(memory_space=pl.ANY),
                  pl.BlockSpec(memory_space=pl.ANY)],
        out_specs=pl.BlockSpec((MOBA_BLOCK, ATTN_W), lambda i: (i, 0)),
        out_shape=jax.ShapeDtypeStruct((t, ATTN_W), BF16),
        scratch_shapes=[pltpu.VMEM(k3.shape, BF16), pltpu.VMEM(v3.shape, BF16),
                        pltpu.VMEM((N_HEADS, 2 * HEAD_DIM, MOBA_BLOCK), BF16),
                        pltpu.VMEM((N_HEADS, nb, MOBA_BLOCK), F32),
                        pltpu.VMEM((N_HEADS * V_ROWS, MOBA_BLOCK), F32),
                        pltpu.VMEM((N_HEADS, MOBA_BLOCK), F32),
                        pltpu.SemaphoreType.DMA((2,))],
        compiler_params=pltpu.CompilerParams(dimension_semantics=("arbitrary",),
                                             vmem_limit_bytes=cfg["attn_vmem"]),
        name="moba_prompt",
    )(qT, kmeans, k3, v3)


def _dec_kernel(pt_ref, qs_ref, kn_ref, vn_ref, ck_hbm, cv_hbm, o_ref,
                kring, kmt_ref, idxv_ref, idxs_ref, kg, vg, vlast, sem_ring, sem_g, sem_misc,
                *, n_pages, nbuf, seq):
    b = pl.program_id(0)
    ppb = MOBA_BLOCK // V7X_LANES
    nblk = n_pages // ppb
    n_sel = seq * MOBA_TOPK * ppb

    def page_copy(p, slot):
        return pltpu.make_async_copy(ck_hbm.at[pt_ref[b, p]], kring.at[slot], sem_ring.at[slot])

    for p in range(nbuf):
        page_copy(p, p).start()

    lane = lax.broadcasted_iota(jnp.int32, (ATTN_W, V7X_LANES), 1)
    kmt_ref[...] = jnp.zeros_like(kmt_ref)

    def blk_body(j, c):
        bs = None
        for pg in range(ppb):
            p = j * ppb + pg
            slot = p % nbuf
            page_copy(p, slot).wait()
            bs = kring[slot] if bs is None else bs + kring[slot]
        colsum = jnp.sum(bs, axis=1, keepdims=True)
        kmt_ref[...] = jnp.where(lane == j, colsum, kmt_ref[...])
        for pg in range(ppb):
            p = j * ppb + pg

            @pl.when(p + nbuf < n_pages)
            def _():
                page_copy(p + nbuf, p % nbuf).start()
        return c

    lax.fori_loop(0, nblk, blk_body, 0)

    kmt = kmt_ref[...] * (1.0 / MOBA_BLOCK)
    sub8 = lax.broadcasted_iota(jnp.int32, (N_HEADS, V7X_LANES), 0)
    lane8 = lax.broadcasted_iota(jnp.int32, (N_HEADS, V7X_LANES), 1)
    idxv = jnp.zeros((N_HEADS, V7X_LANES), jnp.int32)
    for s in range(seq):
        qcol = qs_ref[0, :, s:s + 1]
        g = jnp.full((N_HEADS, V7X_LANES), -jnp.inf, F32)
        for h in range(N_HEADS):
            hs = slice(h * HEAD_DIM, (h + 1) * HEAD_DIM)
            gh = jnp.sum(kmt[hs] * qcol[hs], axis=0, keepdims=True)
            g = jnp.where(jnp.logical_and(sub8 == h, lane8 < nblk), gh, g)
        for r in range(MOBA_TOPK):
            mx = jnp.max(g, axis=1, keepdims=True)
            idx = jnp.min(jnp.where(g == mx, lane8, V7X_LANES), axis=1, keepdims=True)
            idxv = jnp.where(lane8 == s * MOBA_TOPK + r, idx, idxv)
            g = jnp.where(lane8 == idx, -jnp.inf, g)
    idxv_ref[...] = idxv
    cp = pltpu.make_async_copy(idxv_ref, idxs_ref, sem_misc.at[0])
    cp.start()
    cp.wait()

    def tile_copies(n):
        h = n % N_HEADS
        t = n // N_HEADS
        pg = t % ppb
        sr = t // ppb
        page = pt_ref[b, idxs_ref[h, sr] * ppb + pg]
        rows = pl.ds(pl.multiple_of(h * HEAD_DIM, HEAD_DIM), HEAD_DIM)
        return (pltpu.make_async_copy(ck_hbm.at[page, rows], kg.at[t, rows], sem_g.at[0]),
                pltpu.make_async_copy(cv_hbm.at[page, rows], vg.at[t, rows], sem_g.at[1]))

    def issue(n, c):
        ck, cv = tile_copies(n)
        ck.start()
        cv.start()
        return c

    lax.fori_loop(0, n_sel * N_HEADS, issue, 0)
    last = pltpu.make_async_copy(cv_hbm.at[pt_ref[b, n_pages - 1]], vlast, sem_misc.at[1])
    last.start()

    def drain(n, c):
        ck, cv = tile_copies(n)
        ck.wait()
        cv.wait()
        return c

    lax.fori_loop(0, n_sel * N_HEADS, drain, 0)
    last.wait()

    lane_s = lax.broadcasted_iota(jnp.int32, (1, seq), 1)
    per_q = MOBA_TOPK * ppb
    cols = []
    for s in range(seq):
        qcol = qs_ref[0, :, s:s + 1] * SM_SCALE
        outs = []
        for h in range(N_HEADS):
            hs = slice(h * HEAD_DIM, (h + 1) * HEAD_DIM)
            qh = qcol[hs]
            scs = [jnp.sum(kg[s * per_q + t, hs, :] * qh, axis=0, keepdims=True) for t in range(per_q)]
            loc = jnp.sum(kn_ref[0, hs, :] * qh, axis=0, keepdims=True)
            loc = jnp.where(lane_s <= s, loc, NEG)
            mt = scs[0]
            for sc in scs[1:]:
                mt = jnp.maximum(mt, sc)
            m = jnp.maximum(jnp.max(mt, axis=1, keepdims=True), jnp.max(loc, axis=1, keepdims=True))
            ps = [jnp.exp(sc - m) for sc in scs]
            ploc = jnp.exp(loc - m)
            plast = jnp.exp(NEG - m)
            psum = ps[0]
            for p_ in ps[1:]:
                psum = psum + p_
            l = (jnp.sum(psum, axis=1, keepdims=True) + jnp.sum(ploc, axis=1, keepdims=True)
                 + V7X_LANES * plast)
            acc = vlast[hs, :] * plast
            for t in range(per_q):
                acc = acc + vg[s * per_q + t, hs, :] * ps[t]
            o = jnp.sum(acc, axis=1, keepdims=True)
            o = o + jnp.sum(vn_ref[0, hs, :] * ploc, axis=1, keepdims=True)
            outs.append(o / l)
        cols.append(jnp.concatenate(outs, axis=0))
    o_ref[0] = jnp.concatenate(cols, axis=1)


def _decode_attention(page_table, qs, kn, vn, ckT, cvT):
    bsz, _, seq = qs.shape
    n_pages = page_table.shape[1]
    nbuf = min(16, n_pages)
    ppb = MOBA_BLOCK // V7X_LANES
    assert ckT.shape[1:] == (ATTN_W, V7X_LANES) and n_pages % ppb == 0 and nbuf % ppb == 0
    assert n_pages // ppb <= V7X_LANES and seq * MOBA_TOPK <= V7X_LANES
    n_tiles = seq * MOBA_TOPK * ppb
    cfg = _tiles(bsz * seq)
    tok = pl.BlockSpec((1, ATTN_W, seq), lambda b, pt: (b, 0, 0))
    grid_spec = pltpu.PrefetchScalarGridSpec(
        num_scalar_prefetch=1, grid=(bsz,),
        in_specs=[tok, tok, tok, pl.BlockSpec(memory_space=pl.ANY), pl.BlockSpec(memory_space=pl.ANY)],
        out_specs=tok,
        scratch_shapes=[pltpu.VMEM((nbuf, ATTN_W, V7X_LANES), F32),
                        pltpu.VMEM((ATTN_W, V7X_LANES), F32),
                        pltpu.VMEM((N_HEADS, V7X_LANES), jnp.int32),
                        pltpu.SMEM((N_HEADS, V7X_LANES), jnp.int32),
                        pltpu.VMEM((n_tiles, ATTN_W, V7X_LANES), F32),
                        pltpu.VMEM((n_tiles, ATTN_W, V7X_LANES), F32),
                        pltpu.VMEM((ATTN_W, V7X_LANES), F32),
                        pltpu.SemaphoreType.DMA((nbuf,)),
                        pltpu.SemaphoreType.DMA((2,)),
                        pltpu.SemaphoreType.DMA((2,))])
    return pl.pallas_call(
        functools.partial(_dec_kernel, n_pages=n_pages, nbuf=nbuf, seq=seq),
        grid_spec=grid_spec,
        out_shape=jax.ShapeDtypeStruct((bsz, ATTN_W, seq), F32),
        compiler_params=pltpu.CompilerParams(dimension_semantics=("arbitrary",),
                                             vmem_limit_bytes=cfg["dec_vmem"]),
        name="moba_decode",
    )(page_table, qs, kn, vn, ckT, cvT)


def _mlp_kernel(x_ref, a_ref, y_ref, sga_ref, sgc_ref, wba_ref, wbc_ref, wo_ref, g_ref, wup_ref, wdn_ref,
                o_ref, *, chunk):
    a = jnp.dot(a_ref[...], wba_ref[...], preferred_element_type=F32)
    c = jnp.dot(y_ref[...], wbc_ref[...], preferred_element_type=F32)
    mixed = sga_ref[...].astype(F32) * a + sgc_ref[...].astype(F32) * c
    x1 = x_ref[...] + jnp.dot(mixed.astype(BF16), wo_ref[...], preferred_element_type=F32)
    h2 = _rms_rows(x1, g_ref[...]).astype(BF16)
    acc = x1
    for c0 in range(0, wup_ref.shape[1], chunk):
        hid = jnp.maximum(jnp.dot(h2, wup_ref[:, c0:c0 + chunk], preferred_element_type=F32), 0.0)
        acc = acc + jnp.dot((hid * hid).astype(BF16), wdn_ref[c0:c0 + chunk, :], preferred_element_type=F32)
    o_ref[...] = acc


def _merge_mlp(x2d, attn, y, sga, sgc, wba, wbc, wo, g_mlp, wup, wdn):
    n, d = x2d.shape
    cfg = _tiles(n)
    tm = cfg["tm"]
    row = lambda w: pl.BlockSpec((tm, w), lambda i: (i, 0))
    weights = [wba, wbc, wo, g_mlp, wup, wdn]
    return pl.pallas_call(
        functools.partial(_mlp_kernel, chunk=d),
        grid=(n // tm,),
        in_specs=[row(d), row(attn.shape[1]), row(d), row(d), row(d)] + [_resident(w.shape) for w in weights],
        out_specs=row(d),
        out_shape=jax.ShapeDtypeStruct((n, d), F32),
        compiler_params=pltpu.CompilerParams(dimension_semantics=("arbitrary",),
                                             vmem_limit_bytes=cfg["mlp_vmem"]),
        name="merge_mlp",
    )(x2d, attn, y, sga, sgc, *weights)


def _heads_last(xT, lead):
    n = xT.shape[1]
    return jnp.transpose(xT.reshape(N_HEADS, HEAD_DIM, n), (2, 0, 1)).reshape(*lead, N_HEADS, HEAD_DIM)


def kernel(x_prompt, x_sample, cache_k, cache_v, state_conv, page_table, norm_mix, w_in, q_norm, k_norm, conv_w, w_br_attn, w_br_conv, w_out, norm_mlp, w_up, w_down):
    bp, t, d = x_prompt.shape
    db, s, _ = x_sample.shape
    depth = w_in.shape[0]
    n_pool, page = cache_k.shape[1], cache_k.shape[2]
    assert bp == 1 and page == V7X_LANES and t % MOBA_BLOCK == 0
    past = page_table.shape[1] * page
    inv = (ROPE_THETA ** (-jnp.arange(ROT_HALF, dtype=F32) * 2.0 / ROT_DIM)).reshape(ROT_HALF, 1)

    xp = x_prompt.reshape(t, d)
    xs = x_sample.reshape(db * s, d)
    kp_l, vp_l, cp_l, ks_l, vs_l, cs_l = [], [], [], [], [], []
    for l in range(depth):
        w_qkvT = w_in[l][:, :3 * ATTN_W].T.astype(BF16)
        w_rest = w_in[l][:, 3 * ATTN_W:].astype(BF16)
        g_mix = norm_mix[l].reshape(1, d)
        qg = q_norm[l].reshape(HEAD_DIM, 1)
        kg = k_norm[l].reshape(HEAD_DIM, 1)
        cw = conv_w[l]
        weights = (w_br_attn[l].astype(BF16), w_br_conv[l].astype(BF16), w_out[l].astype(BF16),
                   norm_mlp[l].reshape(1, d), w_up[l].astype(BF16), w_down[l].astype(BF16))

        qT, kT, vT, y, sga, sgc, u_tail, k3, v3, km = _project(xp, g_mix, w_qkvT, w_rest, qg, kg, inv, cw, pos0=0)
        attn = _prompt_attention(qT, km.reshape(t // MOBA_BLOCK, ATTN_W), k3, v3)
        xp = _merge_mlp(xp, attn, y, sga, sgc, *weights)
        kp_l.append(_heads_last(kT, (bp, t)))
        vp_l.append(_heads_last(vT, (bp, t)))
        cp_l.append(u_tail[V7X_SUBLANES - (CONV_K - 1):].reshape(bp, CONV_K - 1, d))

        st = state_conv[l]
        zero = jnp.zeros((db, s, d), F32)
        s1 = zero.at[:, 0].set(st[:, 1]).reshape(db * s, d)
        s2 = zero.at[:, 0].set(st[:, 0]).at[:, 1].set(st[:, 1]).reshape(db * s, d)
        qT, kT, vT, y, sga, sgc, u = _project(xs, g_mix, w_qkvT, w_rest, qg, kg, inv, cw, pos0=past,
                                               prev=(s1, s2, s))
        per_seq = lambda xT: jnp.transpose(xT.reshape(ATTN_W, db, s), (1, 0, 2))
        ckT = jnp.transpose(cache_k[l], (0, 2, 3, 1)).reshape(n_pool, ATTN_W, page)
        cvT = jnp.transpose(cache_v[l], (0, 2, 3, 1)).reshape(n_pool, ATTN_W, page)
        oT = _decode_attention(page_table, per_seq(qT), per_seq(kT), per_seq(vT), ckT, cvT)
        attn = jnp.transpose(oT, (0, 2, 1)).reshape(db * s, ATTN_W).astype(BF16)
        xs = _merge_mlp(xs, attn, y, sga, sgc, *weights)
        ks_l.append(_heads_last(kT, (db, s)))
        vs_l.append(_heads_last(vT, (db, s)))
        cs_l.append(u.reshape(db, s, d)[:, s - (CONV_K - 1):])

    return (xp.reshape(bp, t, d), xs.reshape(db, s, d), jnp.stack(kp_l), jnp.stack(vp_l), jnp.stack(cp_l),
            jnp.stack(ks_l), jnp.stack(vs_l), jnp.stack(cs_l))
```

```python
import functools

import jax
import jax.numpy as jnp
from jax import lax
from jax.experimental import pallas as pl
from jax.experimental.pallas import tpu as pltpu

N_HEADS = 8
HEAD_DIM = 64
ATTN_W = N_HEADS * HEAD_DIM
MOBA_BLOCK = 256
MOBA_TOPK = 3
ROT_DIM = HEAD_DIM // 4
ROT_HALF = ROT_DIM // 2
ROPE_THETA = 500000.0
EPS = 1e-6
NEG = -1e30
CONV_K = 3
SM_SCALE = HEAD_DIM ** -0.5
LOG2E = 1.4426950408889634
BIG = 1e30
V_ROWS = HEAD_DIM + 16

V7X_VMEM_BYTES = 64 * 1024 * 1024
V7X_SUBLANES = 8
V7X_LANES = 128
MIB = 1024 * 1024

F32 = jnp.float32
BF16 = jnp.bfloat16
HIGHEST = lax.Precision.HIGHEST


def _tiles(n_rows):
    tm = 512 if n_rows % 512 == 0 else n_rows
    return dict(tm=tm, proj_vmem=56 * MIB, attn_vmem=52 * MIB, mlp_vmem=56 * MIB, dec_vmem=56 * MIB,
                attn_q_split=1, attn_look=8, attn_unroll=4)


def _resident(shape):
    nd = len(shape)
    return pl.BlockSpec(shape, lambda *_: (0,) * nd, pipeline_mode=pl.Buffered(1))


def _rms_rows(x, g):
    ms = jnp.mean(x * x, axis=-1, keepdims=True)
    return x * lax.rsqrt(ms + EPS) * g


def _sigmoid(x):
    return 1.0 / (1.0 + jnp.exp(-x))


def _proj_kernel(*refs, tm, pos0, pos_mask, prompt):
    if prompt:
        (x_ref, g_ref, wt_ref, wr_ref, qg_ref, kg_ref, inv_ref, cw_ref,
         qT_ref, kT_ref, vT_ref, y_ref, sga_ref, sgc_ref, u_ref,
         k3_ref, v3_ref, km_ref, carry_ref) = refs
    else:
        (x_ref, g_ref, wt_ref, wr_ref, qg_ref, kg_ref, inv_ref, cw_ref, s1_ref, s2_ref,
         qT_ref, kT_ref, vT_ref, y_ref, sga_ref, sgc_ref, u_ref) = refs
    i = pl.program_id(0)
    width = x_ref.shape[1]

    h = _rms_rows(x_ref[...], g_ref[...]).astype(BF16)

    zT = lax.dot_general(wt_ref[...], h, (((1,), (1,)), ((), ())), preferred_element_type=F32)

    col = lax.broadcasted_iota(jnp.int32, (1, tm), 1) + i * tm
    if pos_mask is not None:
        col = col & pos_mask
    ang = inv_ref[...] * (col + pos0).astype(F32)
    cos = jnp.cos(ang)
    sin = jnp.sin(ang)

    def norm_rope(zt, gain):
        outs = []
        for hh in range(N_HEADS):
            blk = zt[hh * HEAD_DIM:(hh + 1) * HEAD_DIM]
            ms = jnp.mean(blk * blk, axis=0, keepdims=True)
            blk = blk * lax.rsqrt(ms + EPS) * gain
            x1 = blk[0:ROT_HALF]
            x2 = blk[ROT_HALF:ROT_DIM]
            outs += [x1 * cos - x2 * sin, x2 * cos + x1 * sin, blk[ROT_DIM:HEAD_DIM]]
        return jnp.concatenate(outs, axis=0)

    qT = norm_rope(zT[0:ATTN_W], qg_ref[...])
    kT = norm_rope(zT[ATTN_W:2 * ATTN_W], kg_ref[...])
    vT = zT[2 * ATTN_W:3 * ATTN_W]
    qT_ref[...] = qT
    kT_ref[...] = kT
    vT_ref[...] = vT

    if prompt:
        k_rows = kT.T
        pad_row = lax.broadcasted_iota(jnp.int32, (V_ROWS - HEAD_DIM, MOBA_BLOCK), 0)
        ones_pad = jnp.where(pad_row == 0, 1.0, 0.0).astype(BF16)
        for g in range(tm // MOBA_BLOCK):
            kb = k_rows[g * MOBA_BLOCK:(g + 1) * MOBA_BLOCK]
            km_ref[0, g:g + 1, :] = jnp.mean(kb, axis=0, keepdims=True)
            k3_ref[g] = kb.astype(BF16)
            vb = vT[:, g * MOBA_BLOCK:(g + 1) * MOBA_BLOCK].astype(BF16)
            pieces = []
            for hh in range(N_HEADS):
                pieces += [vb[hh * HEAD_DIM:(hh + 1) * HEAD_DIM], ones_pad]
            v3_ref[g] = jnp.concatenate(pieces, axis=0)

    def seg(c):
        return jnp.dot(h, wr_ref[:, c * width:(c + 1) * width], preferred_element_type=F32)

    u = seg(2) * seg(0)
    if prompt:
        @pl.when(i == 0)
        def _():
            carry_ref[...] = jnp.zeros_like(carry_ref)
        carry = carry_ref[...]
    else:
        carry = jnp.zeros((V7X_SUBLANES, width), F32)
    up = jnp.concatenate([carry, u], axis=0)
    u1 = up[V7X_SUBLANES - 1:V7X_SUBLANES - 1 + tm]
    u2 = up[V7X_SUBLANES - 2:V7X_SUBLANES - 2 + tm]
    if not prompt:
        rm = lax.broadcasted_iota(jnp.int32, (tm, 1), 0) & pos_mask
        u1 = jnp.where(rm >= 1, u1, s1_ref[...])
        u2 = jnp.where(rm >= 2, u2, s2_ref[...])
    cw = cw_ref[...]
    conv = cw[0:1] * u2 + cw[1:2] * u1 + cw[2:3] * u
    y_ref[...] = (seg(1) * conv).astype(BF16)
    sga_ref[...] = _sigmoid(seg(3)).astype(BF16)
    sgc_ref[...] = _sigmoid(seg(4)).astype(BF16)
    if prompt:
        tail = u[tm - V7X_SUBLANES:tm]
        carry_ref[...] = tail
        u_ref[...] = tail
    else:
        u_ref[...] = u


def _project(x2d, g_mix, w_qkvT, w_rest, qg, kg, inv, cw, *, pos0, prev=None):
    n, d = x2d.shape
    cfg = _tiles(n)
    tm = cfg["tm"]
    prompt = prev is None
    nt = n // tm
    row = lambda w: pl.BlockSpec((tm, w), lambda i: (i, 0))
    colT = pl.BlockSpec((ATTN_W, tm), lambda i: (0, i))
    in_specs = [row(d), _resident(g_mix.shape), _resident(w_qkvT.shape), _resident(w_rest.shape),
                _resident(qg.shape), _resident(kg.shape), _resident(inv.shape), _resident(cw.shape)]
    args = [x2d, g_mix, w_qkvT, w_rest, qg, kg, inv, cw]
    out_shape = [jax.ShapeDtypeStruct((ATTN_W, n), F32)] * 3 + [jax.ShapeDtypeStruct((n, d), BF16)] * 3
    out_specs = [colT, colT, colT, row(d), row(d), row(d)]
    scratch = []
    if prompt:
        assert tm % MOBA_BLOCK == 0
        gpt = tm // MOBA_BLOCK
        out_shape += [jax.ShapeDtypeStruct((V7X_SUBLANES, d), F32),
                      jax.ShapeDtypeStruct((n // MOBA_BLOCK, MOBA_BLOCK, ATTN_W), BF16),
                      jax.ShapeDtypeStruct((n // MOBA_BLOCK, N_HEADS * V_ROWS, MOBA_BLOCK), BF16),
                      jax.ShapeDtypeStruct((nt, gpt, ATTN_W), F32)]
        out_specs += [pl.BlockSpec((V7X_SUBLANES, d), lambda i: (0, 0)),
                      pl.BlockSpec((gpt, MOBA_BLOCK, ATTN_W), lambda i: (i, 0, 0)),
                      pl.BlockSpec((gpt, N_HEADS * V_ROWS, MOBA_BLOCK), lambda i: (i, 0, 0)),
                      pl.BlockSpec((1, gpt, ATTN_W), lambda i: (i, 0, 0))]
        scratch = [pltpu.VMEM((V7X_SUBLANES, d), F32)]
        pos_mask = None
    else:
        s1, s2, seq = prev
        assert seq & (seq - 1) == 0
        pos_mask = seq - 1
        in_specs += [row(d), row(d)]
        args += [s1, s2]
        out_shape += [jax.ShapeDtypeStruct((n, d), F32)]
        out_specs += [row(d)]
    return pl.pallas_call(
        functools.partial(_proj_kernel, tm=tm, pos0=pos0, pos_mask=pos_mask, prompt=prompt),
        grid=(nt,), in_specs=in_specs, out_specs=out_specs, out_shape=out_shape, scratch_shapes=scratch,
        compiler_params=pltpu.CompilerParams(dimension_semantics=("arbitrary",),
                                             vmem_limit_bytes=cfg["proj_vmem"]),
        name="proj_prompt" if prompt else "proj_sample",
    )(*args)


def _attn_kernel(q_ref, km_ref, k_hbm, v_hbm, o_ref,
                 kbuf, vbuf, qm_ref, sel_ref, acc_ref, m_ref, sem, *, nb, q_split, look, unroll):
    i = pl.program_id(0)
    blk_q = MOBA_BLOCK

    @pl.when(i == 0)
    def _():
        ck = pltpu.make_async_copy(k_hbm, kbuf, sem.at[0])
        cv = pltpu.make_async_copy(v_hbm, vbuf, sem.at[1])
        ck.start()
        cv.start()
        ck.wait()
        cv.wait()

    pair_w = 2 * HEAD_DIM
    blk = lax.broadcasted_iota(jnp.int32, (nb, blk_q), 0)
    rid = lax.broadcasted_iota(jnp.int32, (pair_w, 1), 0)
    for h in range(N_HEADS):
        pr = h // 2
        qp = q_ref[pr * pair_w:(pr + 1) * pair_w, :]
        keep = (rid >= HEAD_DIM) if h % 2 else (rid < HEAD_DIM)
        qm = jnp.where(keep, qp, 0.0)
        gate = jnp.dot(km_ref[:, pr * pair_w:(pr + 1) * pair_w], qm,
                       precision=HIGHEST, preferred_element_type=F32)
        g = jnp.where(blk < i, gate, -jnp.inf)
        sel = jnp.zeros((nb, blk_q), F32)
        for r in range(MOBA_TOPK):
            mx = jnp.max(g, axis=0, keepdims=True)
            idx = jnp.min(jnp.where(g == mx, blk, nb), axis=0, keepdims=True)
            hit = blk == idx
            sel = jnp.where(jnp.logical_and(hit, i > r), 1.0, sel)
            g = jnp.where(hit, -jnp.inf, g)
        sel_ref[h] = sel
        qm_ref[h] = (qm * (SM_SCALE * LOG2E)).astype(BF16)

    qw = blk_q // q_split

    def scores(j, h, c):
        pr = h // 2
        return jnp.dot(kbuf[j, :, pr * pair_w:(pr + 1) * pair_w], qm_ref[h, :, c * qw:(c + 1) * qw],
                       preferred_element_type=F32)

    def pv(j, h, p):
        return jnp.dot(vbuf[j, h * V_ROWS:(h + 1) * V_ROWS, :], p.astype(BF16), preferred_element_type=F32)

    kq = lax.broadcasted_iota(jnp.int32, (blk_q, qw), 0)
    qq = lax.broadcasted_iota(jnp.int32, (blk_q, qw), 1)

    def run_tiles(js, own):
        tiles = [(j, h, c) for j in js for h in range(N_HEADS) for c in range(q_split)]
        pending = {}
        for k in range(len(tiles) + look):
            if k < len(tiles):
                pending[k] = scores(*tiles[k])
            if k < look:
                continue
            j, h, c = tiles[k - look]
            s = pending.pop(k - look)
            cols = slice(c * qw, (c + 1) * qw)
            rows = slice(h * V_ROWS, (h + 1) * V_ROWS)
            if own:
                s = jnp.where(kq <= qq + c * qw, s, NEG)
                m = jnp.max(s, axis=0, keepdims=True)
                m_ref[h:h + 1, cols] = m
                acc_ref[rows, cols] = pv(j, h, jnp.exp2(s - m))
            else:
                picked = sel_ref[h, pl.ds(j, 1), :][:, cols] > 0.0
                m_old = m_ref[h:h + 1, cols]
                m_new = jnp.where(picked, jnp.maximum(m_old, jnp.max(s, axis=0, keepdims=True)), m_old)
                m_ref[h:h + 1, cols] = m_new
                part = pv(j, h, jnp.exp2(s - jnp.where(picked, m_new, BIG)))
                acc_ref[rows, cols] = jnp.exp2(m_old - m_new) * acc_ref[rows, cols] + part

    run_tiles([i], own=True)

    def past_body(jj, c):
        run_tiles([unroll * jj + b for b in range(unroll)], own=False)
        return c

    lax.fori_loop(0, i // unroll, past_body, 0)
    if unroll > 1:
        def tail_body(j, c):
            run_tiles([j], own=False)
            return c

        lax.fori_loop((i // unroll) * unroll, i, tail_body, 0)

    outs = []
    for h in range(N_HEADS):
        a = acc_ref[h * V_ROWS:(h + 1) * V_ROWS, :]
        outs.append(a[0:HEAD_DIM] / a[HEAD_DIM:HEAD_DIM + 1])
    o_ref[...] = jnp.concatenate(outs, axis=0).T.astype(BF16)


def _prompt_attention(qT, kmeans, k3, v3):
    nb = k3.shape[0]
    t = nb * MOBA_BLOCK
    cfg = _tiles(t)
    return pl.pallas_call(
        functools.partial(_attn_kernel, nb=nb, q_split=cfg["attn_q_split"], look=cfg["attn_look"],
                          unroll=cfg["attn_unroll"]),
        grid=(nb,),
        in_specs=[pl.BlockSpec((ATTN_W, MOBA_BLOCK), lambda i: (0, i)),
                  _resident(kmeans.shape),
                  pl.BlockSpec(memory_space=pl.ANY),
                  pl.BlockSpec(memory_space=pl.ANY)],
        out_specs=pl.BlockSpec((MOBA_BLOCK, ATTN_W), lambda i: (i, 0)),
        out_shape=jax.ShapeDtypeStruct((t, ATTN_W), BF16),
        scratch_shapes=[pltpu.VMEM(k3.shape, BF16), pltpu.VMEM(v3.shape, BF16),
                        pltpu.VMEM((N_HEADS, 2 * HEAD_DIM, MOBA_BLOCK), BF16),
                        pltpu.VMEM((N_HEADS, nb, MOBA_BLOCK), F32),
                        pltpu.VMEM((N_HEADS * V_ROWS, MOBA_BLOCK), F32),
                        pltpu.VMEM((N_HEADS, MOBA_BLOCK), F32),
                        pltpu.SemaphoreType.DMA((2,))],
        compiler_params=pltpu.CompilerParams(dimension_semantics=("arbitrary",),
                                             vmem_limit_bytes=cfg["attn_vmem"]),
        name="moba_prompt",
    )(qT, kmeans, k3, v3)


def _dec_kernel(pt_ref, qs_ref, kn_ref, vn_ref, ck_hbm, cv_hbm, o_ref,
                ring, kmt_ref, idxv_ref, idxs_ref, kg, vg, vlast, sem_ring, sem_g, sem_misc,
                *, n_pages, seq, blk_unroll):
    b = pl.program_id(0)
    n_seq = pl.num_programs(0)
    ppb = MOBA_BLOCK // V7X_LANES
    nblk = n_pages // ppb
    per_q = MOBA_TOPK * ppb
    n_tiles = seq * per_q

    def page_copy(sq, j, pg):
        p = j * ppb + pg
        return pltpu.make_async_copy(ck_hbm.at[pt_ref[sq, p]], ring.at[p], sem_ring.at[j])

    @pl.when(b == 0)
    def _():
        def first(j, c):
            for pg in range(ppb):
                page_copy(0, j, pg).start()
            return c
        lax.fori_loop(0, nblk, first, 0)

    lane = lax.broadcasted_iota(jnp.int32, (ATTN_W, V7X_LANES), 1)
    kmt_ref[...] = jnp.zeros_like(kmt_ref)

    def stream(jj, c):
        js = [jj * blk_unroll + u for u in range(blk_unroll)]
        for j in js:
            for pg in range(ppb):
                page_copy(b, j, pg).wait()
        sums = []
        for j in js:
            bs = ring[j * ppb]
            for pg in range(1, ppb):
                bs = bs + ring[j * ppb + pg]
            sums.append(jnp.sum(bs, axis=1, keepdims=True))

        @pl.when(b + 1 < n_seq)
        def _():
            for j in js:
                for pg in range(ppb):
                    page_copy(b + 1, j, pg).start()
        acc = kmt_ref[...]
        for j, cs in zip(js, sums):
            acc = jnp.where(lane == j, cs, acc)
        kmt_ref[...] = acc
        return c

    lax.fori_loop(0, nblk // blk_unroll, stream, 0)

    def head_rows(fn, width):
        sub = lax.broadcasted_iota(jnp.int32, (N_HEADS, width), 0)
        out = jnp.broadcast_to(fn(0), (N_HEADS, width))
        for h in range(1, N_HEADS):
            out = jnp.where(sub == h, fn(h), out)
        return out

    def hs(h):
        return slice(h * HEAD_DIM, (h + 1) * HEAD_DIM)

    kmt = kmt_ref[...] * (1.0 / MOBA_BLOCK)
    lane8 = lax.broadcasted_iota(jnp.int32, (N_HEADS, V7X_LANES), 1)
    idxv = jnp.zeros((N_HEADS, V7X_LANES), jnp.int32)
    for s in range(seq):
        qcol = qs_ref[0, :, s:s + 1]
        g = head_rows(lambda h: jnp.sum(kmt[hs(h)] * qcol[hs(h)], axis=0, keepdims=True), V7X_LANES)
        g = jnp.where(lane8 < nblk, g, -jnp.inf)
        for r in range(MOBA_TOPK):
            mx = jnp.max(g, axis=1, keepdims=True)
            idx = jnp.min(jnp.where(g == mx, lane8, V7X_LANES), axis=1, keepdims=True)
            idxv = jnp.where(lane8 == s * MOBA_TOPK + r, idx, idxv)
            g = jnp.where(lane8 == idx, -jnp.inf, g)
    idxv_ref[...] = idxv
    cp = pltpu.make_async_copy(idxv_ref, idxs_ref, sem_misc.at[0])
    cp.start()
    cp.wait()

    def tile_copies(t, h, page):
        return (pltpu.make_async_copy(ck_hbm.at[page, hs(h)], kg.at[t, hs(h)], sem_g.at[0]),
                pltpu.make_async_copy(cv_hbm.at[page, hs(h)], vg.at[t, hs(h)], sem_g.at[1]))

    def issue(t, c):
        sr = lax.div(t, ppb)
        pg = t - sr * ppb
        for h in range(N_HEADS):
            ck, cv = tile_copies(t, h, pt_ref[b, idxs_ref[h, sr] * ppb + pg])
            ck.start()
            cv.start()
        return c

    lax.fori_loop(0, n_tiles, issue, 0)
    last = pltpu.make_async_copy(cv_hbm.at[pt_ref[b, n_pages - 1]], vlast, sem_misc.at[1])
    last.start()
    for _ in range(n_tiles * N_HEADS):
        ck, cv = tile_copies(0, 0, 0)
        ck.wait()
        cv.wait()
    last.wait()

    lane_s = lax.broadcasted_iota(jnp.int32, (N_HEADS, seq), 1)
    cols = []
    for s in range(seq):
        qcol = qs_ref[0, :, s:s + 1] * SM_SCALE

        def tile_scores(t, s=s, qcol=qcol):
            return head_rows(lambda h: jnp.sum(kg[s * per_q + t, hs(h), :] * qcol[hs(h)], axis=0, keepdims=True),
                             V7X_LANES)

        scs = [tile_scores(t) for t in range(per_q)]
        loc = head_rows(lambda h: jnp.sum(kn_ref[0, hs(h), :] * qcol[hs(h)], axis=0, keepdims=True), seq)
        loc = jnp.where(lane_s <= s, loc, NEG)
        mt = scs[0]
        for sc in scs[1:]:
            mt = jnp.maximum(mt, sc)
        m = jnp.maximum(jnp.max(mt, axis=1, keepdims=True), jnp.max(loc, axis=1, keepdims=True))
        ps = [jnp.exp(sc - m) for sc in scs]
        ploc = jnp.exp(loc - m)
        plast = jnp.exp(NEG - m)
        psum = ps[0]
        for p_ in ps[1:]:
            psum = psum + p_
        l = jnp.sum(psum, axis=1, keepdims=True) + jnp.sum(ploc, axis=1, keepdims=True) + V7X_LANES * plast
        rl = 1.0 / l
        outs = []
        for h in range(N_HEADS):
            acc = vlast[hs(h), :] * plast[h:h + 1, :]
            for t in range(per_q):
                acc = acc + vg[s * per_q + t, hs(h), :] * ps[t][h:h + 1, :]
            o = jnp.sum(acc, axis=1, keepdims=True)
            o = o + jnp.sum(vn_ref[0, hs(h), :] * ploc[h:h + 1, :], axis=1, keepdims=True)
            outs.append(o * rl[h:h + 1, :])
        cols.append(jnp.concatenate(outs, axis=0))
    o_ref[0] = jnp.concatenate(cols, axis=1)


def _decode_attention(page_table, qs, kn, vn, ckT, cvT):
    bsz, _, seq = qs.shape
    n_pages = page_table.shape[1]
    ppb = MOBA_BLOCK // V7X_LANES
    nblk = n_pages // ppb
    blk_unroll = next(u for u in (4, 2, 1) if nblk % u == 0)
    assert ckT.shape[1:] == (ATTN_W, V7X_LANES) and n_pages % ppb == 0
    assert nblk <= V7X_LANES and seq * MOBA_TOPK <= V7X_LANES
    n_tiles = seq * MOBA_TOPK * ppb
    cfg = _tiles(bsz * seq)
    tok = pl.BlockSpec((1, ATTN_W, seq), lambda b, pt: (b, 0, 0))
    grid_spec = pltpu.PrefetchScalarGridSpec(
        num_scalar_prefetch=1, grid=(bsz,),
        in_specs=[tok, tok, tok, pl.BlockSpec(memory_space=pl.ANY), pl.BlockSpec(memory_space=pl.ANY)],
        out_specs=tok,
        scratch_shapes=[pltpu.VMEM((n_pages, ATTN_W, V7X_LANES), F32),
                        pltpu.VMEM((ATTN_W, V7X_LANES), F32),
                        pltpu.VMEM((N_HEADS, V7X_LANES), jnp.int32),
                        pltpu.SMEM((N_HEADS, V7X_LANES), jnp.int32),
                        pltpu.VMEM((n_tiles, ATTN_W, V7X_LANES), F32),
                        pltpu.VMEM((n_tiles, ATTN_W, V7X_LANES), F32),
                        pltpu.VMEM((ATTN_W, V7X_LANES), F32),
                        pltpu.SemaphoreType.DMA((nblk,)),
                        pltpu.SemaphoreType.DMA((2,)),
                        pltpu.SemaphoreType.DMA((2,))])
    return pl.pallas_call(
        functools.partial(_dec_kernel, n_pages=n_pages, seq=seq, blk_unroll=blk_unroll),
        grid_spec=grid_spec,
        out_shape=jax.ShapeDtypeStruct((bsz, ATTN_W, seq), F32),
        compiler_params=pltpu.CompilerParams(dimension_semantics=("arbitrary",),
                                             vmem_limit_bytes=cfg["dec_vmem"]),
        name="moba_decode",
    )(page_table, qs, kn, vn, ckT, cvT)


def _mlp_kernel(x_ref, a_ref, y_ref, sga_ref, sgc_ref, wba_ref, wbc_ref, wo_ref, g_ref, wup_ref, wdn_ref,
                o_ref, *, chunk):
    a = jnp.dot(a_ref[...], wba_ref[...], preferred_element_type=F32)
    c = jnp.dot(y_ref[...], wbc_ref[...], preferred_element_type=F32)
    mixed = sga_ref[...].astype(F32) * a + sgc_ref[...].astype(F32) * c
    x1 = x_ref[...] + jnp.dot(mixed.astype(BF16), wo_ref[...], preferred_element_type=F32)
    h2 = _rms_rows(x1, g_ref[...]).astype(BF16)
    acc = x1
    for c0 in range(0, wup_ref.shape[1], chunk):
        hid = jnp.maximum(jnp.dot(h2, wup_ref[:, c0:c0 + chunk], preferred_element_type=F32), 0.0)
        acc = acc + jnp.dot((hid * hid).astype(BF16), wdn_ref[c0:c0 + chunk, :], preferred_element_type=F32)
    o_ref[...] = acc


def _merge_mlp(x2d, attn, y, sga, sgc, wba, wbc, wo, g_mlp, wup, wdn):
    n, d = x2d.shape
    cfg = _tiles(n)
    tm = cfg["tm"]
    row = lambda w: pl.BlockSpec((tm, w), lambda i: (i, 0))
    weights = [wba, wbc, wo, g_mlp, wup, wdn]
    return pl.pallas_call(
        functools.partial(_mlp_kernel, chunk=d),
        grid=(n // tm,),
        in_specs=[row(d), row(attn.shape[1]), row(d), row(d), row(d)] + [_resident(w.shape) for w in weights],
        out_specs=row(d),
        out_shape=jax.ShapeDtypeStruct((n, d), F32),
        compiler_params=pltpu.CompilerParams(dimension_semantics=("arbitrary",),
                                             vmem_limit_bytes=cfg["mlp_vmem"]),
        name="merge_mlp",
    )(x2d, attn, y, sga, sgc, *weights)


def _heads_last(xT, lead):
    n = xT.shape[1]
    return jnp.transpose(xT.reshape(N_HEADS, HEAD_DIM, n), (2, 0, 1)).reshape(*lead, N_HEADS, HEAD_DIM)


def kernel(x_prompt, x_sample, cache_k, cache_v, state_conv, page_table, norm_mix, w_in, q_norm, k_norm, conv_w, w_br_attn, w_br_conv, w_out, norm_mlp, w_up, w_down):
    bp, t, d = x_prompt.shape
    db, s, _ = x_sample.shape
    depth = w_in.shape[0]
    n_pool, page = cache_k.shape[1], cache_k.shape[2]
    assert bp == 1 and page == V7X_LANES and t % MOBA_BLOCK == 0
    past = page_table.shape[1] * page
    inv = (ROPE_THETA ** (-jnp.arange(ROT_HALF, dtype=F32) * 2.0 / ROT_DIM)).reshape(ROT_HALF, 1)

    xp = x_prompt.reshape(t, d)
    xs = x_sample.reshape(db * s, d)
    kp_l, vp_l, cp_l, ks_l, vs_l, cs_l = [], [], [], [], [], []
    for l in range(depth):
        w_qkvT = w_in[l][:, :3 * ATTN_W].T.astype(BF16)
        w_rest = w_in[l][:, 3 * ATTN_W:].astype(BF16)
        g_mix = norm_mix[l].reshape(1, d)
        qg = q_norm[l].reshape(HEAD_DIM, 1)
        kg = k_norm[l].reshape(HEAD_DIM, 1)
        cw = conv_w[l]
        weights = (w_br_attn[l].astype(BF16), w_br_conv[l].astype(BF16), w_out[l].astype(BF16),
                   norm_mlp[l].reshape(1, d), w_up[l].astype(BF16), w_down[l].astype(BF16))

        qT, kT, vT, y, sga, sgc, u_tail, k3, v3, km = _project(xp, g_mix, w_qkvT, w_rest, qg, kg, inv, cw, pos0=0)
        attn = _prompt_attention(qT, km.reshape(t // MOBA_BLOCK, ATTN_W), k3, v3)
        xp = _merge_mlp(xp, attn, y, sga, sgc, *weights)
        kp_l.append(_heads_last(kT, (bp, t)))
        vp_l.append(_heads_last(vT, (bp, t)))
        cp_l.append(u_tail[V7X_SUBLANES - (CONV_K - 1):].reshape(bp, CONV_K - 1, d))

        st = state_conv[l]
        zero = jnp.zeros((db, s, d), F32)
        s1 = zero.at[:, 0].set(st[:, 1]).reshape(db * s, d)
        s2 = zero.at[:, 0].set(st[:, 0]).at[:, 1].set(st[:, 1]).reshape(db * s, d)
        qT, kT, vT, y, sga, sgc, u = _project(xs, g_mix, w_qkvT, w_rest, qg, kg, inv, cw, pos0=past,
                                               prev=(s1, s2, s))
        per_seq = lambda xT: jnp.transpose(xT.reshape(ATTN_W, db, s), (1, 0, 2))
        ckT = jnp.transpose(cache_k[l], (0, 2, 3, 1)).reshape(n_pool, ATTN_W, page)
        cvT = jnp.transpose(cache_v[l], (0, 2, 3, 1)).reshape(n_pool, ATTN_W, page)
        oT = _decode_attention(page_table, per_seq(qT), per_seq(kT), per_seq(vT), ckT, cvT)
        attn = jnp.transpose(oT, (0, 2, 1)).reshape(db * s, ATTN_W).astype(BF16)
        xs = _merge_mlp(xs, attn, y, sga, sgc, *weights)
        ks_l.append(_heads_last(kT, (db, s)))
        vs_l.append(_heads_last(vT, (db, s)))
        cs_l.append(u.reshape(db, s, d)[:, s - (CONV_K - 1):])

    return (xp.reshape(bp, t, d), xs.reshape(db, s, d), jnp.stack(kp_l), jnp.stack(vp_l), jnp.stack(cp_l),
            jnp.stack(ks_l), jnp.stack(vs_l), jnp.stack(cs_l))
```

```python
import functools

import jax
import jax.numpy as jnp
from jax import lax
from jax.experimental import pallas as pl
from jax.experimental.pallas import tpu as pltpu

N_HEADS = 8
HEAD_DIM = 64
ATTN_W = N_HEADS * HEAD_DIM
MOBA_BLOCK = 256
MOBA_TOPK = 3
ROT_DIM = HEAD_DIM // 4
ROT_HALF = ROT_DIM // 2
ROPE_THETA = 500000.0
EPS = 1e-6
NEG = -1e30
CONV_K = 3
SM_SCALE = HEAD_DIM ** -0.5
LOG2E = 1.4426950408889634
BIG = 1e30
V_ROWS = HEAD_DIM + 16

V7X_VMEM_BYTES = 64 * 1024 * 1024
V7X_SUBLANES = 8
V7X_LANES = 128
MIB = 1024 * 1024

F32 = jnp.float32
BF16 = jnp.bfloat16
HIGHEST = lax.Precision.HIGHEST


def _tiles(n_rows):
    tm = 512 if n_rows % 512 == 0 else n_rows
    return dict(tm=tm, proj_vmem=56 * MIB, attn_vmem=52 * MIB, mlp_vmem=56 * MIB, dec_vmem=56 * MIB,
                attn_q_split=1, attn_look=6, attn_unroll=4)


def _resident(shape):
    nd = len(shape)
    return pl.BlockSpec(shape, lambda *_: (0,) * nd, pipeline_mode=pl.Buffered(1))


def _rms_rows(x, g):
    ms = jnp.mean(x * x, axis=-1, keepdims=True)
    return x * lax.rsqrt(ms + EPS) * g


def _sigmoid(x):
    return 1.0 / (1.0 + jnp.exp(-x))


def _proj_kernel(*refs, tm, pos0, pos_mask, prompt):
    if prompt:
        (x_ref, g_ref, wt_ref, wr_ref, qg_ref, kg_ref, inv_ref, cw_ref,
         qT_ref, kT_ref, vT_ref, y_ref, sga_ref, sgc_ref, u_ref,
         k3_ref, v3_ref, km_ref, carry_ref) = refs
    else:
        (x_ref, g_ref, wt_ref, wr_ref, qg_ref, kg_ref, inv_ref, cw_ref, s1_ref, s2_ref,
         qT_ref, kT_ref, vT_ref, y_ref, sga_ref, sgc_ref, u_ref) = refs
    i = pl.program_id(0)
    width = x_ref.shape[1]

    h = _rms_rows(x_ref[...], g_ref[...]).astype(BF16)

    zT = lax.dot_general(wt_ref[...], h, (((1,), (1,)), ((), ())), preferred_element_type=F32)

    col = lax.broadcasted_iota(jnp.int32, (1, tm), 1) + i * tm
    if pos_mask is not None:
        col = col & pos_mask
    ang = inv_ref[...] * (col + pos0).astype(F32)
    cos = jnp.cos(ang)
    sin = jnp.sin(ang)

    def norm_rope(zt, gain):
        outs = []
        for hh in range(N_HEADS):
            blk = zt[hh * HEAD_DIM:(hh + 1) * HEAD_DIM]
            ms = jnp.mean(blk * blk, axis=0, keepdims=True)
            blk = blk * lax.rsqrt(ms + EPS) * gain
            x1 = blk[0:ROT_HALF]
            x2 = blk[ROT_HALF:ROT_DIM]
            outs += [x1 * cos - x2 * sin, x2 * cos + x1 * sin, blk[ROT_DIM:HEAD_DIM]]
        return jnp.concatenate(outs, axis=0)

    qT = norm_rope(zT[0:ATTN_W], qg_ref[...])
    kT = norm_rope(zT[ATTN_W:2 * ATTN_W], kg_ref[...])
    vT = zT[2 * ATTN_W:3 * ATTN_W]
    qT_ref[...] = qT
    kT_ref[...] = kT
    vT_ref[...] = vT

    if prompt:
        k_rows = kT.T
        pad_row = lax.broadcasted_iota(jnp.int32, (V_ROWS - HEAD_DIM, MOBA_BLOCK), 0)
        ones_pad = jnp.where(pad_row == 0, 1.0, 0.0).astype(BF16)
        for g in range(tm // MOBA_BLOCK):
            kb = k_rows[g * MOBA_BLOCK:(g + 1) * MOBA_BLOCK]
            km_ref[0, g:g + 1, :] = jnp.mean(kb, axis=0, keepdims=True)
            k3_ref[g] = kb.astype(BF16)
            vb = vT[:, g * MOBA_BLOCK:(g + 1) * MOBA_BLOCK].astype(BF16)
            pieces = []
            for hh in range(N_HEADS):
                pieces += [vb[hh * HEAD_DIM:(hh + 1) * HEAD_DIM], ones_pad]
            v3_ref[g] = jnp.concatenate(pieces, axis=0)

    def seg(c):
        return jnp.dot(h, wr_ref[:, c * width:(c + 1) * width], preferred_element_type=F32)

    u = seg(2) * seg(0)
    if prompt:
        @pl.when(i == 0)
        def _():
            carry_ref[...] = jnp.zeros_like(carry_ref)
        carry = carry_ref[...]
    else:
        carry = jnp.zeros((V7X_SUBLANES, width), F32)
    up = jnp.concatenate([carry, u], axis=0)
    u1 = up[V7X_SUBLANES - 1:V7X_SUBLANES - 1 + tm]
    u2 = up[V7X_SUBLANES - 2:V7X_SUBLANES - 2 + tm]
    if not prompt:
        rm = lax.broadcasted_iota(jnp.int32, (tm, 1), 0) & pos_mask
        u1 = jnp.where(rm >= 1, u1, s1_ref[...])
        u2 = jnp.where(rm >= 2, u2, s2_ref[...])
    cw = cw_ref[...]
    conv = cw[0:1] * u2 + cw[1:2] * u1 + cw[2:3] * u
    y_ref[...] = (seg(1) * conv).astype(BF16)
    sga_ref[...] = _sigmoid(seg(3)).astype(BF16)
    sgc_ref[...] = _sigmoid(seg(4)).astype(BF16)
    if prompt:
        tail = u[tm - V7X_SUBLANES:tm]
        carry_ref[...] = tail
        u_ref[...] = tail
    else:
        u_ref[...] = u


def _project(x2d, g_mix, w_qkvT, w_rest, qg, kg, inv, cw, *, pos0, prev=None):
    n, d = x2d.shape
    cfg = _tiles(n)
    tm = cfg["tm"]
    prompt = prev is None
    nt = n // tm
    row = lambda w: pl.BlockSpec((tm, w), lambda i: (i, 0))
    colT = pl.BlockSpec((ATTN_W, tm), lambda i: (0, i))
    in_specs = [row(d), _resident(g_mix.shape), _resident(w_qkvT.shape), _resident(w_rest.shape),
                _resident(qg.shape), _resident(kg.shape), _resident(inv.shape), _resident(cw.shape)]
    args = [x2d, g_mix, w_qkvT, w_rest, qg, kg, inv, cw]
    out_shape = [jax.ShapeDtypeStruct((ATTN_W, n), F32)] * 3 + [jax.ShapeDtypeStruct((n, d), BF16)] * 3
    out_specs = [colT, colT, colT, row(d), row(d), row(d)]
    scratch = []
    if prompt:
        assert tm % MOBA_BLOCK == 0
        gpt = tm // MOBA_BLOCK
        out_shape += [jax.ShapeDtypeStruct((V7X_SUBLANES, d), F32),
                      jax.ShapeDtypeStruct((n // MOBA_BLOCK, MOBA_BLOCK, ATTN_W), BF16),
                      jax.ShapeDtypeStruct((n // MOBA_BLOCK, N_HEADS * V_ROWS, MOBA_BLOCK), BF16),
                      jax.ShapeDtypeStruct((nt, gpt, ATTN_W), F32)]
        out_specs += [pl.BlockSpec((V7X_SUBLANES, d), lambda i: (0, 0)),
                      pl.BlockSpec((gpt, MOBA_BLOCK, ATTN_W), lambda i: (i, 0, 0)),
                      pl.BlockSpec((gpt, N_HEADS * V_ROWS, MOBA_BLOCK), lambda i: (i, 0, 0)),
                      pl.BlockSpec((1, gpt, ATTN_W), lambda i: (i, 0, 0))]
        scratch = [pltpu.VMEM((V7X_SUBLANES, d), F32)]
        pos_mask = None
    else:
        s1, s2, seq = prev
        assert seq & (seq - 1) == 0
        pos_mask = seq - 1
        in_specs += [row(d), row(d)]
        args += [s1, s2]
        out_shape += [jax.ShapeDtypeStruct((n, d), F32)]
        out_specs += [row(d)]
    return pl.pallas_call(
        functools.partial(_proj_kernel, tm=tm, pos0=pos0, pos_mask=pos_mask, prompt=prompt),
        grid=(nt,), in_specs=in_specs, out_specs=out_specs, out_shape=out_shape, scratch_shapes=scratch,
        compiler_params=pltpu.CompilerParams(dimension_semantics=("arbitrary",),
                                             vmem_limit_bytes=cfg["proj_vmem"]),
        name="proj_prompt" if prompt else "proj_sample",
    )(*args)


def _attn_kernel(q_ref, km_ref, k_hbm, v_hbm, o_ref,
                 kbuf, vbuf, qm_ref, sel_ref, acc_ref, m_ref, sem, *, nb, q_split, look, unroll):
    i = pl.program_id(0)
    blk_q = MOBA_BLOCK

    @pl.when(i == 0)
    def _():
        ck = pltpu.make_async_copy(k_hbm, kbuf, sem.at[0])
        cv = pltpu.make_async_copy(v_hbm, vbuf, sem.at[1])
        ck.start()
        cv.start()
        ck.wait()
        cv.wait()

    pair_w = 2 * HEAD_DIM
    blk = lax.broadcasted_iota(jnp.int32, (nb, blk_q), 0)
    rid = lax.broadcasted_iota(jnp.int32, (pair_w, 1), 0)
    for h in range(N_HEADS):
        pr = h // 2
        qp = q_ref[pr * pair_w:(pr + 1) * pair_w, :]
        keep = (rid >= HEAD_DIM) if h % 2 else (rid < HEAD_DIM)
        qm = jnp.where(keep, qp, 0.0)
        gate = jnp.dot(km_ref[:, pr * pair_w:(pr + 1) * pair_w], qm,
                       precision=HIGHEST, preferred_element_type=F32)
        g = jnp.where(blk < i, gate, -jnp.inf)
        sel = jnp.zeros((nb, blk_q), F32)
        for r in range(MOBA_TOPK):
            mx = jnp.max(g, axis=0, keepdims=True)
            idx = jnp.min(jnp.where(g == mx, blk, nb), axis=0, keepdims=True)
            hit = blk == idx
            sel = jnp.where(jnp.logical_and(hit, i > r), 1.0, sel)
            g = jnp.where(hit, -jnp.inf, g)
        sel_ref[h] = sel
        qm_ref[h] = (qm * (SM_SCALE * LOG2E)).astype(BF16)

    qw = blk_q // q_split

    def scores(j, h, c):
        pr = h // 2
        return jnp.dot(kbuf[j, :, pr * pair_w:(pr + 1) * pair_w], qm_ref[h, :, c * qw:(c + 1) * qw],
                       preferred_element_type=F32)

    def pv(j, h, p):
        return jnp.dot(vbuf[j, h * V_ROWS:(h + 1) * V_ROWS, :], p.astype(BF16), preferred_element_type=F32)

    kq = lax.broadcasted_iota(jnp.int32, (blk_q, qw), 0)
    qq = lax.broadcasted_iota(jnp.int32, (blk_q, qw), 1)

    def run_tiles(js, own):
        tiles = [(j, h, c) for j in js for h in range(N_HEADS) for c in range(q_split)]
        pending = {}
        for k in range(len(tiles) + look):
            if k < len(tiles):
                pending[k] = scores(*tiles[k])
            if k < look:
                continue
            j, h, c = tiles[k - look]
            s = pending.pop(k - look)
            cols = slice(c * qw, (c + 1) * qw)
            rows = slice(h * V_ROWS, (h + 1) * V_ROWS)
            if own:
                s = jnp.where(kq <= qq + c * qw, s, NEG)
                m = jnp.max(s, axis=0, keepdims=True)
                m_ref[h:h + 1, cols] = m
                acc_ref[rows, cols] = pv(j, h, jnp.exp2(s - m))
            else:
                picked = sel_ref[h, pl.ds(j, 1), :][:, cols] > 0.0
                m_old = m_ref[h:h + 1, cols]
                m_new = jnp.where(picked, jnp.maximum(m_old, jnp.max(s, axis=0, keepdims=True)), m_old)
                m_ref[h:h + 1, cols] = m_new
                part = pv(j, h, jnp.exp2(s - jnp.where(picked, m_new, BIG)))
                acc_ref[rows, cols] = jnp.exp2(m_old - m_new) * acc_ref[rows, cols] + part

    run_tiles([i], own=True)

    def past_body(jj, c):
        run_tiles([unroll * jj + b for b in range(unroll)], own=False)
        return c

    lax.fori_loop(0, i // unroll, past_body, 0)
    if unroll > 1:
        def tail_body(j, c):
            run_tiles([j], own=False)
            return c

        lax.fori_loop((i // unroll) * unroll, i, tail_body, 0)

    outs = []
    for h in range(N_HEADS):
        a = acc_ref[h * V_ROWS:(h + 1) * V_ROWS, :]
        outs.append(a[0:HEAD_DIM] / a[HEAD_DIM:HEAD_DIM + 1])
    o_ref[...] = jnp.concatenate(outs, axis=0).T.astype(BF16)


def _prompt_attention(qT, kmeans, k3, v3):
    nb = k3.shape[0]
    t = nb * MOBA_BLOCK
    cfg = _tiles(t)
    return pl.pallas_call(
        functools.partial(_attn_kernel, nb=nb, q_split=cfg["attn_q_split"], look=cfg["attn_look"],
                          unroll=cfg["attn_unroll"]),
        grid=(nb,),
        in_specs=[pl.BlockSpec((ATTN_W, MOBA_BLOCK), lambda i: (0, i)),
                  _resident(kmeans.shape),
                  pl.BlockSpec(memory_space=pl.ANY),
                  pl.BlockSpec(memory_space=pl.ANY)],
        out_specs=pl.BlockSpec((MOBA_BLOCK, ATTN_W), lambda i: (i, 0)),
        out_shape=jax.ShapeDtypeStruct((t, ATTN_W), BF16),
        scratch_shapes=[pltpu.VMEM(k3.shape, BF16), pltpu.VMEM(v3.shape, BF16),
                        pltpu.VMEM((N_HEADS, 2 * HEAD_DIM, MOBA_BLOCK), BF16),
                        pltpu.VMEM((N_HEADS, nb, MOBA_BLOCK), F32),
                        pltpu.VMEM((N_HEADS * V_ROWS, MOBA_BLOCK), F32),
                        pltpu.VMEM((N_HEADS, MOBA_BLOCK), F32),
                        pltpu.SemaphoreType.DMA((2,))],
        compiler_params=pltpu.CompilerParams(dimension_semantics=("arbitrary",),
                                             vmem_limit_bytes=cfg["attn_vmem"]),
        name="moba_prompt",
    )(qT, kmeans, k3, v3)


def _dec_kernel(pt_ref, qs_ref, kn_ref, vn_ref, ck_hbm, cv_hbm, o_ref,
                ring, kmt_ref, idxv_ref, idxs_ref, kg, vg, vlast, sem_ring, sem_g, sem_misc,
                *, n_pages, seq, blk_unroll):
    b = pl.program_id(0)
    n_seq = pl.num_programs(0)
    ppb = MOBA_BLOCK // V7X_LANES
    nblk = n_pages // ppb
    per_q = MOBA_TOPK * ppb
    n_tiles = seq * per_q

    def page_copy(sq, j, pg):
        p = j * ppb + pg
        return pltpu.make_async_copy(ck_hbm.at[pt_ref[sq, p]], ring.at[p], sem_ring.at[j])

    @pl.when(b == 0)
    def _():
        def first(j, c):
            for pg in range(ppb):
                page_copy(0, j, pg).start()
            return c
        lax.fori_loop(0, nblk, first, 0)

    lane = lax.broadcasted_iota(jnp.int32, (ATTN_W, V7X_LANES), 1)
    kmt_ref[...] = jnp.zeros_like(kmt_ref)

    def stream(jj, c):
        js = [jj * blk_unroll + u for u in range(blk_unroll)]
        for j in js:
            for pg in range(ppb):
                page_copy(b, j, pg).wait()
        sums = []
        for j in js:
            bs = ring[j * ppb]
            for pg in range(1, ppb):
                bs = bs + ring[j * ppb + pg]
            sums.append(jnp.sum(bs, axis=1, keepdims=True))

        @pl.when(b + 1 < n_seq)
        def _():
            for j in js:
                for pg in range(ppb):
                    page_copy(b + 1, j, pg).start()
        acc = kmt_ref[...]
        for j, cs in zip(js, sums):
            acc = jnp.where(lane == j, cs, acc)
        kmt_ref[...] = acc
        return c

    lax.fori_loop(0, nblk // blk_unroll, stream, 0)

    def head_rows(fn, width):
        sub = lax.broadcasted_iota(jnp.int32, (N_HEADS, width), 0)
        out = jnp.broadcast_to(fn(0), (N_HEADS, width))
        for h in range(1, N_HEADS):
            out = jnp.where(sub == h, fn(h), out)
        return out

    def hs(h):
        return slice(h * HEAD_DIM, (h + 1) * HEAD_DIM)

    kmt = kmt_ref[...] * (1.0 / MOBA_BLOCK)
    lane8 = lax.broadcasted_iota(jnp.int32, (N_HEADS, V7X_LANES), 1)
    idxv = jnp.zeros((N_HEADS, V7X_LANES), jnp.int32)
    for s in range(seq):
        qcol = qs_ref[0, :, s:s + 1]
        g = head_rows(lambda h: jnp.sum(kmt[hs(h)] * qcol[hs(h)], axis=0, keepdims=True), V7X_LANES)
        g = jnp.where(lane8 < nblk, g, -jnp.inf)
        for r in range(MOBA_TOPK):
            mx = jnp.max(g, axis=1, keepdims=True)
            idx = jnp.min(jnp.where(g == mx, lane8, V7X_LANES), axis=1, keepdims=True)
            idxv = jnp.where(lane8 == s * MOBA_TOPK + r, idx, idxv)
            g = jnp.where(lane8 == idx, -jnp.inf, g)
    idxv_ref[...] = idxv
    cp = pltpu.make_async_copy(idxv_ref, idxs_ref, sem_misc.at[0])
    cp.start()
    cp.wait()

    def tile_copies(t, h, page):
        return (pltpu.make_async_copy(ck_hbm.at[page, hs(h)], kg.at[t, hs(h)], sem_g.at[0]),
                pltpu.make_async_copy(cv_hbm.at[page, hs(h)], vg.at[t, hs(h)], sem_g.at[1]))

    def issue(t, c):
        sr = lax.div(t, ppb)
        pg = t - sr * ppb
        for h in range(N_HEADS):
            ck, cv = tile_copies(t, h, pt_ref[b, idxs_ref[h, sr] * ppb + pg])
            ck.start(priority=1)
            cv.start(priority=1)
        return c

    lax.fori_loop(0, n_tiles, issue, 0)
    last = pltpu.make_async_copy(cv_hbm.at[pt_ref[b, n_pages - 1]], vlast, sem_misc.at[1])
    last.start(priority=1)
    for _ in range(n_tiles * N_HEADS):
        ck, cv = tile_copies(0, 0, 0)
        ck.wait()
        cv.wait()
    last.wait()

    lane_s = lax.broadcasted_iota(jnp.int32, (N_HEADS, seq), 1)
    cols = []
    for s in range(seq):
        qcol = qs_ref[0, :, s:s + 1] * SM_SCALE

        def tile_scores(t, s=s, qcol=qcol):
            return head_rows(lambda h: jnp.sum(kg[s * per_q + t, hs(h), :] * qcol[hs(h)], axis=0, keepdims=True),
                             V7X_LANES)

        scs = [tile_scores(t) for t in range(per_q)]
        loc = head_rows(lambda h: jnp.sum(kn_ref[0, hs(h), :] * qcol[hs(h)], axis=0, keepdims=True), seq)
        loc = jnp.where(lane_s <= s, loc, NEG)
        mt = scs[0]
        for sc in scs[1:]:
            mt = jnp.maximum(mt, sc)
        m = jnp.maximum(jnp.max(mt, axis=1, keepdims=True), jnp.max(loc, axis=1, keepdims=True))
        ps = [jnp.exp(sc - m) for sc in scs]
        ploc = jnp.exp(loc - m)
        plast = jnp.exp(NEG - m)
        psum = ps[0]
        for p_ in ps[1:]:
            psum = psum + p_
        l = jnp.sum(psum, axis=1, keepdims=True) + jnp.sum(ploc, axis=1, keepdims=True) + V7X_LANES * plast
        rl = 1.0 / l
        outs = []
        for h in range(N_HEADS):
            acc = vlast[hs(h), :] * plast[h:h + 1, :]
            for t in range(per_q):
                acc = acc + vg[s * per_q + t, hs(h), :] * ps[t][h:h + 1, :]
            o = jnp.sum(acc, axis=1, keepdims=True)
            o = o + jnp.sum(vn_ref[0, hs(h), :] * ploc[h:h + 1, :], axis=1, keepdims=True)
            outs.append(o * rl[h:h + 1, :])
        cols.append(jnp.concatenate(outs, axis=0))
    o_ref[0] = jnp.concatenate(cols, axis=1)


def _decode_attention(page_table, qs, kn, vn, ckT, cvT):
    bsz, _, seq = qs.shape
    n_pages = page_table.shape[1]
    ppb = MOBA_BLOCK // V7X_LANES
    nblk = n_pages // ppb
    blk_unroll = next(u for u in (4, 2, 1) if nblk % u == 0)
    assert ckT.shape[1:] == (ATTN_W, V7X_LANES) and n_pages % ppb == 0
    assert nblk <= V7X_LANES and seq * MOBA_TOPK <= V7X_LANES
    n_tiles = seq * MOBA_TOPK * ppb
    cfg = _tiles(bsz * seq)
    tok = pl.BlockSpec((1, ATTN_W, seq), lambda b, pt: (b, 0, 0))
    grid_spec = pltpu.PrefetchScalarGridSpec(
        num_scalar_prefetch=1, grid=(bsz,),
        in_specs=[tok, tok, tok, pl.BlockSpec(memory_space=pl.ANY), pl.BlockSpec(memory_space=pl.ANY)],
        out_specs=tok,
        scratch_shapes=[pltpu.VMEM((n_pages, ATTN_W, V7X_LANES), F32),
                        pltpu.VMEM((ATTN_W, V7X_LANES), F32),
                        pltpu.VMEM((N_HEADS, V7X_LANES), jnp.int32),
                        pltpu.SMEM((N_HEADS, V7X_LANES), jnp.int32),
                        pltpu.VMEM((n_tiles, ATTN_W, V7X_LANES), F32),
                        pltpu.VMEM((n_tiles, ATTN_W, V7X_LANES), F32),
                        pltpu.VMEM((ATTN_W, V7X_LANES), F32),
                        pltpu.SemaphoreType.DMA((nblk,)),
                        pltpu.SemaphoreType.DMA((2,)),
                        pltpu.SemaphoreType.DMA((2,))])
    return pl.pallas_call(
        functools.partial(_dec_kernel, n_pages=n_pages, seq=seq, blk_unroll=blk_unroll),
        grid_spec=grid_spec,
        out_shape=jax.ShapeDtypeStruct((bsz, ATTN_W, seq), F32),
        compiler_params=pltpu.CompilerParams(dimension_semantics=("arbitrary",),
                                             vmem_limit_bytes=cfg["dec_vmem"]),
        name="moba_decode",
    )(page_table, qs, kn, vn, ckT, cvT)


def _mlp_kernel(x_ref, a_ref, y_ref, sga_ref, sgc_ref, wba_ref, wbc_ref, wo_ref, g_ref, wup_ref, wdn_ref,
                o_ref, *, chunk):
    a = jnp.dot(a_ref[...], wba_ref[...], preferred_element_type=F32)
    c = jnp.dot(y_ref[...], wbc_ref[...], preferred_element_type=F32)
    mixed = sga_ref[...].astype(F32) * a + sgc_ref[...].astype(F32) * c
    x1 = x_ref[...] + jnp.dot(mixed.astype(BF16), wo_ref[...], preferred_element_type=F32)
    h2 = _rms_rows(x1, g_ref[...]).astype(BF16)
    acc = x1
    for c0 in range(0, wup_ref.shape[1], chunk):
        hid = jnp.maximum(jnp.dot(h2, wup_ref[:, c0:c0 + chunk], preferred_element_type=F32), 0.0)
        acc = acc + jnp.dot((hid * hid).astype(BF16), wdn_ref[c0:c0 + chunk, :], preferred_element_type=F32)
    o_ref[...] = acc


def _merge_mlp(x2d, attn, y, sga, sgc, wba, wbc, wo, g_mlp, wup, wdn):
    n, d = x2d.shape
    cfg = _tiles(n)
    tm = cfg["tm"]
    row = lambda w: pl.BlockSpec((tm, w), lambda i: (i, 0))
    weights = [wba, wbc, wo, g_mlp, wup, wdn]
    return pl.pallas_call(
        functools.partial(_mlp_kernel, chunk=d),
        grid=(n // tm,),
        in_specs=[row(d), row(attn.shape[1]), row(d), row(d), row(d)] + [_resident(w.shape) for w in weights],
        out_specs=row(d),
        out_shape=jax.ShapeDtypeStruct((n, d), F32),
        compiler_params=pltpu.CompilerParams(dimension_semantics=("arbitrary",),
                                             vmem_limit_bytes=cfg["mlp_vmem"]),
        name="merge_mlp",
    )(x2d, attn, y, sga, sgc, *weights)


def _heads_last(xT, lead):
    n = xT.shape[1]
    return jnp.transpose(xT.reshape(N_HEADS, HEAD_DIM, n), (2, 0, 1)).reshape(*lead, N_HEADS, HEAD_DIM)


def kernel(x_prompt, x_sample, cache_k, cache_v, state_conv, page_table, norm_mix, w_in, q_norm, k_norm, conv_w, w_br_attn, w_br_conv, w_out, norm_mlp, w_up, w_down):
    bp, t, d = x_prompt.shape
    db, s, _ = x_sample.shape
    depth = w_in.shape[0]
    n_pool, page = cache_k.shape[1], cache_k.shape[2]
    assert bp == 1 and page == V7X_LANES and t % MOBA_BLOCK == 0
    past = page_table.shape[1] * page
    inv = (ROPE_THETA ** (-jnp.arange(ROT_HALF, dtype=F32) * 2.0 / ROT_DIM)).reshape(ROT_HALF, 1)

    xp = x_prompt.reshape(t, d)
    xs = x_sample.reshape(db * s, d)
    kp_l, vp_l, cp_l, ks_l, vs_l, cs_l = [], [], [], [], [], []
    for l in range(depth):
        w_qkvT = w_in[l][:, :3 * ATTN_W].T.astype(BF16)
        w_rest = w_in[l][:, 3 * ATTN_W:].astype(BF16)
        g_mix = norm_mix[l].reshape(1, d)
        qg = q_norm[l].reshape(HEAD_DIM, 1)
        kg = k_norm[l].reshape(HEAD_DIM, 1)
        cw = conv_w[l]
        weights = (w_br_attn[l].astype(BF16), w_br_conv[l].astype(BF16), w_out[l].astype(BF16),
                   norm_mlp[l].reshape(1, d), w_up[l].astype(BF16), w_down[l].astype(BF16))

        qT, kT, vT, y, sga, sgc, u_tail, k3, v3, km = _project(xp, g_mix, w_qkvT, w_rest, qg, kg, inv, cw, pos0=0)
        attn = _prompt_attention(qT, km.reshape(t // MOBA_BLOCK, ATTN_W), k3, v3)
        xp = _merge_mlp(xp, attn, y, sga, sgc, *weights)
        kp_l.append(_heads_last(kT, (bp, t)))
        vp_l.append(_heads_last(vT, (bp, t)))
        cp_l.append(u_tail[V7X_SUBLANES - (CONV_K - 1):].reshape(bp, CONV_K - 1, d))

        st = state_conv[l]
        zero = jnp.zeros((db, s, d), F32)
        s1 = zero.at[:, 0].set(st[:, 1]).reshape(db * s, d)
        s2 = zero.at[:, 0].set(st[:, 0]).at[:, 1].set(st[:, 1]).reshape(db * s, d)
        qT, kT, vT, y, sga, sgc, u = _project(xs, g_mix, w_qkvT, w_rest, qg, kg, inv, cw, pos0=past,
                                               prev=(s1, s2, s))
        per_seq = lambda xT: jnp.transpose(xT.reshape(ATTN_W, db, s), (1, 0, 2))
        ckT = jnp.transpose(cache_k[l], (0, 2, 3, 1)).reshape(n_pool, ATTN_W, page)
        cvT = jnp.transpose(cache_v[l], (0, 2, 3, 1)).reshape(n_pool, ATTN_W, page)
        oT = _decode_attention(page_table, per_seq(qT), per_seq(kT), per_seq(vT), ckT, cvT)
        attn = jnp.transpose(oT, (0, 2, 1)).reshape(db * s, ATTN_W).astype(BF16)
        xs = _merge_mlp(xs, attn, y, sga, sgc, *weights)
        ks_l.append(_heads_last(kT, (db, s)))
        vs_l.append(_heads_last(vT, (db, s)))
        cs_l.append(u.reshape(db, s, d)[:, s - (CONV_K - 1):])

    return (xp.reshape(bp, t, d), xs.reshape(db, s, d), jnp.stack(kp_l), jnp.stack(vp_l), jnp.stack(cp_l),
            jnp.stack(ks_l), jnp.stack(vs_l), jnp.stack(cs_l))
```

```python
import functools

import jax
import jax.numpy as jnp
from jax import lax
from jax.experimental import pallas as pl
from jax.experimental.pallas import tpu as pltpu
from jax.experimental.pallas import tpu_sc as plsc

N_HEADS = 8
HEAD_DIM = 64
ATTN_W = N_HEADS * HEAD_DIM
MOBA_BLOCK = 256
MOBA_TOPK = 3
ROT_DIM = HEAD_DIM // 4
ROT_HALF = ROT_DIM // 2
ROPE_THETA = 500000.0
EPS = 1e-6
NEG = -1e30
CONV_K = 3
SM_SCALE = HEAD_DIM ** -0.5
LOG2E = 1.4426950408889634
BIG = 1e30
V_ROWS = HEAD_DIM + 16

V7X_VMEM_BYTES = 64 * 1024 * 1024
V7X_SUBLANES = 8
V7X_LANES = 128
MIB = 1024 * 1024
V7X_SC_CORES = 2
V7X_SC_SUBCORES = 16
V7X_SC_LANES = 16
SC_PIECE_ROWS = 128

F32 = jnp.float32
BF16 = jnp.bfloat16
HIGHEST = lax.Precision.HIGHEST


def _tiles(n_rows):
    tm = 512 if n_rows % 512 == 0 else n_rows
    return dict(tm=tm, proj_vmem=56 * MIB, attn_vmem=52 * MIB, mlp_vmem=56 * MIB, dec_vmem=56 * MIB,
                attn_q_split=1, attn_look=6, attn_unroll=4, proj_cols=256)


def _resident(shape):
    nd = len(shape)
    return pl.BlockSpec(shape, lambda *_: (0,) * nd, pipeline_mode=pl.Buffered(1))


def _rms_rows(x, g):
    ms = jnp.mean(x * x, axis=-1, keepdims=True)
    return x * lax.rsqrt(ms + EPS) * g


def _sigmoid(x):
    return 1.0 / (1.0 + jnp.exp(-x))


def _proj_kernel(*refs, tm, pos0, pos_mask, prompt, cols):
    if prompt:
        (x_ref, g_ref, wt_ref, wr_ref, qg_ref, kg_ref, inv_ref, cw_ref,
         qT_ref, kT_ref, vT_ref, y_ref, sga_ref, sgc_ref, u_ref,
         k3_ref, v3_ref, km_ref, carry_ref) = refs
    else:
        (x_ref, g_ref, wt_ref, wr_ref, qg_ref, kg_ref, inv_ref, cw_ref, s1_ref, s2_ref,
         qT_ref, kT_ref, vT_ref, y_ref, sga_ref, sgc_ref, u_ref) = refs
    i = pl.program_id(0)
    width = x_ref.shape[1]

    h = _rms_rows(x_ref[...], g_ref[...]).astype(BF16)

    zT = lax.dot_general(wt_ref[...], h, (((1,), (1,)), ((), ())), preferred_element_type=F32)

    col = lax.broadcasted_iota(jnp.int32, (1, tm), 1) + i * tm
    if pos_mask is not None:
        col = col & pos_mask
    ang = inv_ref[...] * (col + pos0).astype(F32)
    cos = jnp.cos(ang)
    sin = jnp.sin(ang)

    def norm_rope(zt, gain):
        outs = []
        for hh in range(N_HEADS):
            blk = zt[hh * HEAD_DIM:(hh + 1) * HEAD_DIM]
            ms = jnp.mean(blk * blk, axis=0, keepdims=True)
            blk = blk * lax.rsqrt(ms + EPS) * gain
            x1 = blk[0:ROT_HALF]
            x2 = blk[ROT_HALF:ROT_DIM]
            outs += [x1 * cos - x2 * sin, x2 * cos + x1 * sin, blk[ROT_DIM:HEAD_DIM]]
        return jnp.concatenate(outs, axis=0)

    qT = norm_rope(zT[0:ATTN_W], qg_ref[...])
    kT = norm_rope(zT[ATTN_W:2 * ATTN_W], kg_ref[...])
    vT = zT[2 * ATTN_W:3 * ATTN_W]
    qT_ref[...] = qT
    kT_ref[...] = kT
    vT_ref[...] = vT

    if prompt:
        k_rows = kT.T
        pad_row = lax.broadcasted_iota(jnp.int32, (V_ROWS - HEAD_DIM, MOBA_BLOCK), 0)
        ones_pad = jnp.where(pad_row == 0, 1.0, 0.0).astype(BF16)
        for g in range(tm // MOBA_BLOCK):
            kb = k_rows[g * MOBA_BLOCK:(g + 1) * MOBA_BLOCK]
            km_ref[0, g:g + 1, :] = jnp.mean(kb, axis=0, keepdims=True)
            k3_ref[g] = kb.astype(BF16)
            vb = vT[:, g * MOBA_BLOCK:(g + 1) * MOBA_BLOCK].astype(BF16)
            pieces = []
            for hh in range(N_HEADS):
                pieces += [vb[hh * HEAD_DIM:(hh + 1) * HEAD_DIM], ones_pad]
            v3_ref[g] = jnp.concatenate(pieces, axis=0)

    if prompt:
        @pl.when(i == 0)
        def _():
            carry_ref[...] = jnp.zeros_like(carry_ref)
    else:
        rm = lax.broadcasted_iota(jnp.int32, (tm, 1), 0) & pos_mask
    for c0 in range(0, width, cols):
        cs = slice(c0, c0 + cols)

        def seg(k, c0=c0):
            return jnp.dot(h, wr_ref[:, k * width + c0:k * width + c0 + cols], preferred_element_type=F32)

        u = seg(2) * seg(0)
        carry = carry_ref[:, cs] if prompt else jnp.zeros((V7X_SUBLANES, cols), F32)
        up = jnp.concatenate([carry, u], axis=0)
        u1 = up[V7X_SUBLANES - 1:V7X_SUBLANES - 1 + tm]
        u2 = up[V7X_SUBLANES - 2:V7X_SUBLANES - 2 + tm]
        if not prompt:
            u1 = jnp.where(rm >= 1, u1, s1_ref[:, cs])
            u2 = jnp.where(rm >= 2, u2, s2_ref[:, cs])
        conv = cw_ref[0:1, cs] * u2 + cw_ref[1:2, cs] * u1 + cw_ref[2:3, cs] * u
        y_ref[:, cs] = (seg(1) * conv).astype(BF16)
        sga_ref[:, cs] = _sigmoid(seg(3)).astype(BF16)
        sgc_ref[:, cs] = _sigmoid(seg(4)).astype(BF16)
        if prompt:
            tail = u[tm - V7X_SUBLANES:tm]
            carry_ref[:, cs] = tail
            u_ref[:, cs] = tail
        else:
            u_ref[:, cs] = u


def _project(x2d, g_mix, w_qkvT, w_rest, qg, kg, inv, cw, *, pos0, prev=None):
    n, d = x2d.shape
    cfg = _tiles(n)
    tm = cfg["tm"]
    prompt = prev is None
    nt = n // tm
    row = lambda w: pl.BlockSpec((tm, w), lambda i: (i, 0))
    colT = pl.BlockSpec((ATTN_W, tm), lambda i: (0, i))
    in_specs = [row(d), _resident(g_mix.shape), _resident(w_qkvT.shape), _resident(w_rest.shape),
                _resident(qg.shape), _resident(kg.shape), _resident(inv.shape), _resident(cw.shape)]
    args = [x2d, g_mix, w_qkvT, w_rest, qg, kg, inv, cw]
    out_shape = [jax.ShapeDtypeStruct((ATTN_W, n), F32)] * 3 + [jax.ShapeDtypeStruct((n, d), BF16)] * 3
    out_specs = [colT, colT, colT, row(d), row(d), row(d)]
    scratch = []
    if prompt:
        assert tm % MOBA_BLOCK == 0
        gpt = tm // MOBA_BLOCK
        out_shape += [jax.ShapeDtypeStruct((V7X_SUBLANES, d), F32),
                      jax.ShapeDtypeStruct((n // MOBA_BLOCK, MOBA_BLOCK, ATTN_W), BF16),
                      jax.ShapeDtypeStruct((n // MOBA_BLOCK, N_HEADS * V_ROWS, MOBA_BLOCK), BF16),
                      jax.ShapeDtypeStruct((nt, gpt, ATTN_W), F32)]
        out_specs += [pl.BlockSpec((V7X_SUBLANES, d), lambda i: (0, 0)),
                      pl.BlockSpec((gpt, MOBA_BLOCK, ATTN_W), lambda i: (i, 0, 0)),
                      pl.BlockSpec((gpt, N_HEADS * V_ROWS, MOBA_BLOCK), lambda i: (i, 0, 0)),
                      pl.BlockSpec((1, gpt, ATTN_W), lambda i: (i, 0, 0))]
        scratch = [pltpu.VMEM((V7X_SUBLANES, d), F32)]
        pos_mask = None
    else:
        s1, s2, seq = prev
        assert seq & (seq - 1) == 0
        pos_mask = seq - 1
        in_specs += [row(d), row(d)]
        args += [s1, s2]
        out_shape += [jax.ShapeDtypeStruct((n, d), F32)]
        out_specs += [row(d)]
    return pl.pallas_call(
        functools.partial(_proj_kernel, tm=tm, pos0=pos0, pos_mask=pos_mask, prompt=prompt,
                          cols=cfg["proj_cols"]),
        grid=(nt,), in_specs=in_specs, out_specs=out_specs, out_shape=out_shape, scratch_shapes=scratch,
        compiler_params=pltpu.CompilerParams(dimension_semantics=("arbitrary",),
                                             vmem_limit_bytes=cfg["proj_vmem"]),
        name="proj_prompt" if prompt else "proj_sample",
    )(*args)


def _attn_kernel(q_ref, km_ref, k_hbm, v_hbm, o_ref,
                 kbuf, vbuf, qm_ref, sel_ref, acc_ref, m_ref, sem, *, nb, q_split, look, unroll):
    i = pl.program_id(0)
    blk_q = MOBA_BLOCK

    @pl.when(i == 0)
    def _():
        ck = pltpu.make_async_copy(k_hbm, kbuf, sem.at[0])
        cv = pltpu.make_async_copy(v_hbm, vbuf, sem.at[1])
        ck.start()
        cv.start()
        ck.wait()
        cv.wait()

    pair_w = 2 * HEAD_DIM
    blk = lax.broadcasted_iota(jnp.int32, (nb, blk_q), 0)
    rid = lax.broadcasted_iota(jnp.int32, (pair_w, 1), 0)
    for h in range(N_HEADS):
        pr = h // 2
        qp = q_ref[pr * pair_w:(pr + 1) * pair_w, :]
        keep = (rid >= HEAD_DIM) if h % 2 else (rid < HEAD_DIM)
        qm = jnp.where(keep, qp, 0.0)
        gate = jnp.dot(km_ref[:, pr * pair_w:(pr + 1) * pair_w], qm,
                       precision=HIGHEST, preferred_element_type=F32)
        g = jnp.where(blk < i, gate, -jnp.inf)
        sel = jnp.zeros((nb, blk_q), F32)
        for r in range(MOBA_TOPK):
            mx = jnp.max(g, axis=0, keepdims=True)
            idx = jnp.min(jnp.where(g == mx, blk, nb), axis=0, keepdims=True)
            hit = blk == idx
            sel = jnp.where(jnp.logical_and(hit, i > r), 1.0, sel)
            g = jnp.where(hit, -jnp.inf, g)
        sel_ref[h] = sel
        qm_ref[h] = (qm * (SM_SCALE * LOG2E)).astype(BF16)

    qw = blk_q // q_split

    def scores(j, h, c):
        pr = h // 2
        return jnp.dot(kbuf[j, :, pr * pair_w:(pr + 1) * pair_w], qm_ref[h, :, c * qw:(c + 1) * qw],
                       preferred_element_type=F32)

    def pv(j, h, p):
        return jnp.dot(vbuf[j, h * V_ROWS:(h + 1) * V_ROWS, :], p.astype(BF16), preferred_element_type=F32)

    kq = lax.broadcasted_iota(jnp.int32, (blk_q, qw), 0)
    qq = lax.broadcasted_iota(jnp.int32, (blk_q, qw), 1)

    def run_tiles(js, own):
        tiles = [(j, h, c) for j in js for h in range(N_HEADS) for c in range(q_split)]
        pending = {}
        for k in range(len(tiles) + look):
            if k < len(tiles):
                pending[k] = scores(*tiles[k])
            if k < look:
                continue
            j, h, c = tiles[k - look]
            s = pending.pop(k - look)
            cols = slice(c * qw, (c + 1) * qw)
            rows = slice(h * V_ROWS, (h + 1) * V_ROWS)
            if own:
                s = jnp.where(kq <= qq + c * qw, s, NEG)
                m = jnp.max(s, axis=0, keepdims=True)
                m_ref[h:h + 1, cols] = m
                acc_ref[rows, cols] = pv(j, h, jnp.exp2(s - m))
            else:
                picked = sel_ref[h, pl.ds(j, 1), :][:, cols] > 0.0
                m_old = m_ref[h:h + 1, cols]
                m_new = jnp.where(picked, jnp.maximum(m_old, jnp.max(s, axis=0, keepdims=True)), m_old)
                m_ref[h:h + 1, cols] = m_new
                part = pv(j, h, jnp.exp2(s - jnp.where(picked, m_new, BIG)))
                acc_ref[rows, cols] = jnp.exp2(m_old - m_new) * acc_ref[rows, cols] + part

    run_tiles([i], own=True)

    def past_body(jj, c):
        run_tiles([unroll * jj + b for b in range(unroll)], own=False)
        return c

    lax.fori_loop(0, i // unroll, past_body, 0)
    if unroll > 1:
        def tail_body(j, c):
            run_tiles([j], own=False)
            return c

        lax.fori_loop((i // unroll) * unroll, i, tail_body, 0)

    outs = []
    for h in range(N_HEADS):
        a = acc_ref[h * V_ROWS:(h + 1) * V_ROWS, :]
        outs.append(a[0:HEAD_DIM] / a[HEAD_DIM:HEAD_DIM + 1])
    o_ref[...] = jnp.concatenate(outs, axis=0).T.astype(BF16)


def _prompt_attention(qT, kmeans, k3, v3):
    nb = k3.shape[0]
    t = nb * MOBA_BLOCK
    cfg = _tiles(t)
    return pl.pallas_call(
        functools.partial(_attn_kernel, nb=nb, q_split=cfg["attn_q_split"], look=cfg["attn_look"],
                          unroll=cfg["attn_unroll"]),
        grid=(nb,),
        in_specs=[pl.BlockSpec((ATTN_W, MOBA_BLOCK), lambda i: (0, i)),
                  _resident(kmeans.shape),
                  pl.BlockSpec(memory_space=pl.ANY),
                  pl.BlockSpec(memory_space=pl.ANY)],
        out_specs=pl.BlockSpec((MOBA_BLOCK, ATTN_W), lambda i: (i, 0)),
        out_shape=jax.ShapeDtypeStruct((t, ATTN_W), BF16),
        scratch_shapes=[pltpu.VMEM(k3.shape, BF16), pltpu.VMEM(v3.shape, BF16),
                        pltpu.VMEM((N_HEADS, 2 * HEAD_DIM, MOBA_BLOCK), BF16),
                        pltpu.VMEM((N_HEADS, nb, MOBA_BLOCK), F32),
                        pltpu.VMEM((N_HEADS * V_ROWS, MOBA_BLOCK), F32),
                        pltpu.VMEM((N_HEADS, MOBA_BLOCK), F32),
                        pltpu.SemaphoreType.DMA((2,))],
        compiler_params=pltpu.CompilerParams(dimension_semantics=("arbitrary",),
                                             vmem_limit_bytes=cfg["attn_vmem"]),
        name="moba_prompt",
    )(qT, kmeans, k3, v3)


def _dec_kernel(pt_ref, qs_ref, kn_ref, vn_ref, ksum_ref, ck_hbm, cv_hbm, o_ref,
                idxv_ref, idxs_ref, kg, vg, vlast, sem_g, sem_misc, *, n_pages, seq):
    b = pl.program_id(0)
    ppb = MOBA_BLOCK // V7X_LANES
    nblk = n_pages // ppb
    per_q = MOBA_TOPK * ppb
    n_tiles = seq * per_q

    def head_rows(fn, width):
        sub = lax.broadcasted_iota(jnp.int32, (N_HEADS, width), 0)
        out = jnp.broadcast_to(fn(0), (N_HEADS, width))
        for h in range(1, N_HEADS):
            out = jnp.where(sub == h, fn(h), out)
        return out

    def hs(h):
        return slice(h * HEAD_DIM, (h + 1) * HEAD_DIM)

    kmt = ksum_ref[0] * (1.0 / MOBA_BLOCK)
    lane8 = lax.broadcasted_iota(jnp.int32, (N_HEADS, V7X_LANES), 1)
    idxv = jnp.zeros((N_HEADS, V7X_LANES), jnp.int32)
    for s in range(seq):
        qcol = qs_ref[0, :, s:s + 1]
        g = head_rows(lambda h: jnp.sum(kmt[hs(h)] * qcol[hs(h)], axis=0, keepdims=True), V7X_LANES)
        g = jnp.where(lane8 < nblk, g, -jnp.inf)
        for r in range(MOBA_TOPK):
            mx = jnp.max(g, axis=1, keepdims=True)
            idx = jnp.min(jnp.where(g == mx, lane8, V7X_LANES), axis=1, keepdims=True)
            idxv = jnp.where(lane8 == s * MOBA_TOPK + r, idx, idxv)
            g = jnp.where(lane8 == idx, -jnp.inf, g)
    idxv_ref[...] = idxv
    cp = pltpu.make_async_copy(idxv_ref, idxs_ref, sem_misc.at[0])
    cp.start()
    cp.wait()

    def tile_copies(t, h, page):
        return (pltpu.make_async_copy(ck_hbm.at[page, hs(h)], kg.at[t, hs(h)], sem_g.at[0]),
                pltpu.make_async_copy(cv_hbm.at[page, hs(h)], vg.at[t, hs(h)], sem_g.at[1]))

    def issue(t, c):
        sr = lax.div(t, ppb)
        pg = t - sr * ppb
        for h in range(N_HEADS):
            ck, cv = tile_copies(t, h, pt_ref[b, idxs_ref[h, sr] * ppb + pg])
            ck.start(priority=1)
            cv.start(priority=1)
        return c

    lax.fori_loop(0, n_tiles, issue, 0)
    last = pltpu.make_async_copy(cv_hbm.at[pt_ref[b, n_pages - 1]], vlast, sem_misc.at[1])
    last.start(priority=1)
    for _ in range(n_tiles * N_HEADS):
        ck, cv = tile_copies(0, 0, 0)
        ck.wait()
        cv.wait()
    last.wait()

    lane_s = lax.broadcasted_iota(jnp.int32, (N_HEADS, seq), 1)
    cols = []
    for s in range(seq):
        qcol = qs_ref[0, :, s:s + 1] * SM_SCALE

        def tile_scores(t, s=s, qcol=qcol):
            return head_rows(lambda h: jnp.sum(kg[s * per_q + t, hs(h), :] * qcol[hs(h)], axis=0, keepdims=True),
                             V7X_LANES)

        scs = [tile_scores(t) for t in range(per_q)]
        loc = head_rows(lambda h: jnp.sum(kn_ref[0, hs(h), :] * qcol[hs(h)], axis=0, keepdims=True), seq)
        loc = jnp.where(lane_s <= s, loc, NEG)
        mt = scs[0]
        for sc in scs[1:]:
            mt = jnp.maximum(mt, sc)
        m = jnp.maximum(jnp.max(mt, axis=1, keepdims=True), jnp.max(loc, axis=1, keepdims=True))
        ps = [jnp.exp(sc - m) for sc in scs]
        ploc = jnp.exp(loc - m)
        plast = jnp.exp(NEG - m)
        psum = ps[0]
        for p_ in ps[1:]:
            psum = psum + p_
        l = jnp.sum(psum, axis=1, keepdims=True) + jnp.sum(ploc, axis=1, keepdims=True) + V7X_LANES * plast
        rl = 1.0 / l
        outs = []
        for h in range(N_HEADS):
            acc = vlast[hs(h), :] * plast[h:h + 1, :]
            for t in range(per_q):
                acc = acc + vg[s * per_q + t, hs(h), :] * ps[t][h:h + 1, :]
            o = jnp.sum(acc, axis=1, keepdims=True)
            o = o + jnp.sum(vn_ref[0, hs(h), :] * ploc[h:h + 1, :], axis=1, keepdims=True)
            outs.append(o * rl[h:h + 1, :])
        cols.append(jnp.concatenate(outs, axis=0))
    o_ref[0] = jnp.concatenate(cols, axis=1)


def _decode_attention(page_table, qs, kn, vn, ksum, ckT, cvT):
    bsz, _, seq = qs.shape
    n_pages = page_table.shape[1]
    ppb = MOBA_BLOCK // V7X_LANES
    nblk = n_pages // ppb
    assert ckT.shape[1:] == (ATTN_W, V7X_LANES) and n_pages % ppb == 0
    assert nblk <= V7X_LANES and seq * MOBA_TOPK <= V7X_LANES and ksum.shape == (bsz, ATTN_W, V7X_LANES)
    n_tiles = seq * MOBA_TOPK * ppb
    cfg = _tiles(bsz * seq)
    tok = pl.BlockSpec((1, ATTN_W, seq), lambda b, pt: (b, 0, 0))
    grid_spec = pltpu.PrefetchScalarGridSpec(
        num_scalar_prefetch=1, grid=(bsz,),
        in_specs=[tok, tok, tok, pl.BlockSpec((1, ATTN_W, V7X_LANES), lambda b, pt: (b, 0, 0)),
                  pl.BlockSpec(memory_space=pl.ANY), pl.BlockSpec(memory_space=pl.ANY)],
        out_specs=tok,
        scratch_shapes=[pltpu.VMEM((N_HEADS, V7X_LANES), jnp.int32),
                        pltpu.SMEM((N_HEADS, V7X_LANES), jnp.int32),
                        pltpu.VMEM((n_tiles, ATTN_W, V7X_LANES), F32),
                        pltpu.VMEM((n_tiles, ATTN_W, V7X_LANES), F32),
                        pltpu.VMEM((ATTN_W, V7X_LANES), F32),
                        pltpu.SemaphoreType.DMA((2,)),
                        pltpu.SemaphoreType.DMA((2,))])
    return pl.pallas_call(
        functools.partial(_dec_kernel, n_pages=n_pages, seq=seq),
        grid_spec=grid_spec,
        out_shape=jax.ShapeDtypeStruct((bsz, ATTN_W, seq), F32),
        compiler_params=pltpu.CompilerParams(dimension_semantics=("arbitrary",),
                                             vmem_limit_bytes=cfg["dec_vmem"]),
        name="moba_decode",
    )(page_table, qs, kn, vn, ksum, ckT, cvT)


def _sc_block_sums(page_table, ckT):
    bsz, n_pages = page_table.shape
    n_pool = ckT.shape[0]
    ppb = MOBA_BLOCK // V7X_LANES
    nblk = n_pages // ppb
    n_workers = V7X_SC_CORES * V7X_SC_SUBCORES
    assert bsz % n_workers == 0 and ATTN_W % SC_PIECE_ROWS == 0
    seq_per_worker = bsz // n_workers
    pieces = ATTN_W // SC_PIECE_ROWS
    steps = nblk * pieces
    table = ckT.reshape(n_pool * pieces, SC_PIECE_ROWS, V7X_LANES)
    idx = (page_table.reshape(bsz, nblk, 1, ppb) * pieces
           + jnp.arange(pieces, dtype=jnp.int32).reshape(1, 1, pieces, 1)).reshape(-1)
    lanes_per_row = V7X_LANES // V7X_SC_LANES

    def body(idx_hbm, tab_hbm, out_hbm, idx_v, buf, orow, sem):
        wid = lax.axis_index("c") * V7X_SC_SUBCORES + lax.axis_index("s")
        lane = lax.iota(jnp.int32, V7X_SC_LANES)

        def fetch(n, slot):
            return pltpu.make_async_copy(tab_hbm.at[idx_v.at[pl.ds(n * ppb, ppb)]], buf.at[slot], sem.at[slot])

        for q in range(seq_per_worker):
            sq = wid * seq_per_worker + q
            pltpu.sync_copy(idx_hbm.at[pl.ds(sq * (steps * ppb), steps * ppb)], idx_v)
            fetch(0, 0).start()

            @pl.loop(0, steps, step=2)
            def _(n0):
                for slot in range(2):
                    n = n0 + slot

                    @pl.when(n + 1 < steps)
                    def _():
                        fetch(n + 1, 1 - slot).start()

                    fetch(n, slot).wait()

                    @pl.loop(0, SC_PIECE_ROWS, step=V7X_SC_LANES)
                    def _(r0):
                        vec = jnp.zeros((V7X_SC_LANES,), F32)
                        for i in range(V7X_SC_LANES):
                            acc = None
                            for pg in range(ppb):
                                for k in range(lanes_per_row):
                                    v = buf[slot, pg, r0 + i, pl.ds(k * V7X_SC_LANES, V7X_SC_LANES)]
                                    acc = v if acc is None else acc + v
                            vec = jnp.where(lane == i, jnp.sum(acc), vec)
                        orow[pl.ds(r0, V7X_SC_LANES)] = vec

                    pltpu.sync_copy(orow, out_hbm.at[pl.ds((sq * steps + n) * SC_PIECE_ROWS, SC_PIECE_ROWS)])

    sums = pl.kernel(
        body,
        out_type=jax.ShapeDtypeStruct((bsz * steps * SC_PIECE_ROWS,), F32),
        mesh=plsc.VectorSubcoreMesh(core_axis_name="c", subcore_axis_name="s"),
        scratch_types=[pltpu.VMEM((steps * ppb,), jnp.int32),
                       pltpu.VMEM((2, ppb, SC_PIECE_ROWS, V7X_LANES), F32),
                       pltpu.VMEM((SC_PIECE_ROWS,), F32),
                       pltpu.SemaphoreType.DMA((2,))],
        compiler_params=pltpu.CompilerParams(needs_layout_passes=False),
        name="sc_block_sums",
    )(idx, table)
    ksum = jnp.transpose(sums.reshape(bsz, nblk, ATTN_W), (0, 2, 1))
    return jnp.pad(ksum, ((0, 0), (0, 0), (0, V7X_LANES - nblk)))


def _mlp_kernel(x_ref, a_ref, y_ref, sga_ref, sgc_ref, wba_ref, wbc_ref, wo_ref, g_ref, wup_ref, wdn_ref,
                o_ref, *, chunk):
    a = jnp.dot(a_ref[...], wba_ref[...], preferred_element_type=F32)
    c = jnp.dot(y_ref[...], wbc_ref[...], preferred_element_type=F32)
    mixed = sga_ref[...].astype(F32) * a + sgc_ref[...].astype(F32) * c
    x1 = x_ref[...] + jnp.dot(mixed.astype(BF16), wo_ref[...], preferred_element_type=F32)
    h2 = _rms_rows(x1, g_ref[...]).astype(BF16)
    acc = x1
    for c0 in range(0, wup_ref.shape[1], chunk):
        hid = jnp.maximum(jnp.dot(h2, wup_ref[:, c0:c0 + chunk], preferred_element_type=F32), 0.0)
        acc = acc + jnp.dot((hid * hid).astype(BF16), wdn_ref[c0:c0 + chunk, :], preferred_element_type=F32)
    o_ref[...] = acc


def _merge_mlp(x2d, attn, y, sga, sgc, wba, wbc, wo, g_mlp, wup, wdn):
    n, d = x2d.shape
    cfg = _tiles(n)
    tm = cfg["tm"]
    row = lambda w: pl.BlockSpec((tm, w), lambda i: (i, 0))
    weights = [wba, wbc, wo, g_mlp, wup, wdn]
    return pl.pallas_call(
        functools.partial(_mlp_kernel, chunk=d),
        grid=(n // tm,),
        in_specs=[row(d), row(attn.shape[1]), row(d), row(d), row(d)] + [_resident(w.shape) for w in weights],
        out_specs=row(d),
        out_shape=jax.ShapeDtypeStruct((n, d), F32),
        compiler_params=pltpu.CompilerParams(dimension_semantics=("arbitrary",),
                                             vmem_limit_bytes=cfg["mlp_vmem"]),
        name="merge_mlp",
    )(x2d, attn, y, sga, sgc, *weights)


def _heads_last(xT, lead):
    n = xT.shape[1]
    return jnp.transpose(xT.reshape(N_HEADS, HEAD_DIM, n), (2, 0, 1)).reshape(*lead, N_HEADS, HEAD_DIM)


def kernel(x_prompt, x_sample, cache_k, cache_v, state_conv, page_table, norm_mix, w_in, q_norm, k_norm, conv_w, w_br_attn, w_br_conv, w_out, norm_mlp, w_up, w_down):
    bp, t, d = x_prompt.shape
    db, s, _ = x_sample.shape
    depth = w_in.shape[0]
    n_pool, page = cache_k.shape[1], cache_k.shape[2]
    assert bp == 1 and page == V7X_LANES and t % MOBA_BLOCK == 0
    past = page_table.shape[1] * page
    inv = (ROPE_THETA ** (-jnp.arange(ROT_HALF, dtype=F32) * 2.0 / ROT_DIM)).reshape(ROT_HALF, 1)

    ckT_l = [jnp.transpose(cache_k[l], (0, 2, 3, 1)).reshape(n_pool, ATTN_W, page) for l in range(depth)]
    cvT_l = [jnp.transpose(cache_v[l], (0, 2, 3, 1)).reshape(n_pool, ATTN_W, page) for l in range(depth)]
    ksum = [_sc_block_sums(page_table, ckT_l[l]) for l in range(depth)]

    xp = x_prompt.reshape(t, d)
    xs = x_sample.reshape(db * s, d)
    kp_l, vp_l, cp_l, ks_l, vs_l, cs_l = [], [], [], [], [], []
    for l in range(depth):
        w_qkvT = w_in[l][:, :3 * ATTN_W].T.astype(BF16)
        w_rest = w_in[l][:, 3 * ATTN_W:].astype(BF16)
        g_mix = norm_mix[l].reshape(1, d)
        qg = q_norm[l].reshape(HEAD_DIM, 1)
        kg = k_norm[l].reshape(HEAD_DIM, 1)
        cw = conv_w[l]
        weights = (w_br_attn[l].astype(BF16), w_br_conv[l].astype(BF16), w_out[l].astype(BF16),
                   norm_mlp[l].reshape(1, d), w_up[l].astype(BF16), w_down[l].astype(BF16))

        qT, kT, vT, y, sga, sgc, u_tail, k3, v3, km = _project(xp, g_mix, w_qkvT, w_rest, qg, kg, inv, cw, pos0=0)
        attn = _prompt_attention(qT, km.reshape(t // MOBA_BLOCK, ATTN_W), k3, v3)
        xp = _merge_mlp(xp, attn, y, sga, sgc, *weights)
        kp_l.append(_heads_last(kT, (bp, t)))
        vp_l.append(_heads_last(vT, (bp, t)))
        cp_l.append(u_tail[V7X_SUBLANES - (CONV_K - 1):].reshape(bp, CONV_K - 1, d))

        st = state_conv[l]
        zero = jnp.zeros((db, s, d), F32)
        s1 = zero.at[:, 0].set(st[:, 1]).reshape(db * s, d)
        s2 = zero.at[:, 0].set(st[:, 0]).at[:, 1].set(st[:, 1]).reshape(db * s, d)
        qT, kT, vT, y, sga, sgc, u = _project(xs, g_mix, w_qkvT, w_rest, qg, kg, inv, cw, pos0=past,
                                               prev=(s1, s2, s))
        per_seq = lambda xT: jnp.transpose(xT.reshape(ATTN_W, db, s), (1, 0, 2))
        oT = _decode_attention(page_table, per_seq(qT), per_seq(kT), per_seq(vT), ksum[l], ckT_l[l], cvT_l[l])
        attn = jnp.transpose(oT, (0, 2, 1)).reshape(db * s, ATTN_W).astype(BF16)
        xs = _merge_mlp(xs, attn, y, sga, sgc, *weights)
        ks_l.append(_heads_last(kT, (db, s)))
        vs_l.append(_heads_last(vT, (db, s)))
        cs_l.append(u.reshape(db, s, d)[:, s - (CONV_K - 1):])

    return (xp.reshape(bp, t, d), xs.reshape(db, s, d), jnp.stack(kp_l), jnp.stack(vp_l), jnp.stack(cp_l),
            jnp.stack(ks_l), jnp.stack(vs_l), jnp.stack(cs_l))
```

```python
import functools

import jax
import jax.numpy as jnp
from jax import lax
from jax.experimental import pallas as pl
from jax.experimental.pallas import tpu as pltpu
from jax.experimental.pallas import tpu_sc as plsc

N_HEADS = 8
HEAD_DIM = 64
ATTN_W = N_HEADS * HEAD_DIM
MOBA_BLOCK = 256
MOBA_TOPK = 3
ROT_DIM = HEAD_DIM // 4
ROT_HALF = ROT_DIM // 2
ROPE_THETA = 500000.0
EPS = 1e-6
NEG = -1e30
CONV_K = 3
SM_SCALE = HEAD_DIM ** -0.5
LOG2E = 1.4426950408889634
BIG = 1e30
V_ROWS = HEAD_DIM + 16

V7X_VMEM_BYTES = 64 * 1024 * 1024
V7X_SUBLANES = 8
V7X_LANES = 128
MIB = 1024 * 1024
V7X_SC_CORES = 2
V7X_SC_SUBCORES = 16
V7X_SC_LANES = 16
SC_PIECE_ROWS = 128

F32 = jnp.float32
BF16 = jnp.bfloat16
HIGHEST = lax.Precision.HIGHEST


def _tiles(n_rows):
    tm = 512 if n_rows % 512 == 0 else n_rows
    return dict(tm=tm, proj_vmem=56 * MIB, attn_vmem=52 * MIB, mlp_vmem=56 * MIB, dec_vmem=56 * MIB,
                attn_q_split=1, attn_look=6, attn_unroll=4, proj_cols=256)


def _resident(shape):
    nd = len(shape)
    return pl.BlockSpec(shape, lambda *_: (0,) * nd, pipeline_mode=pl.Buffered(1))


def _rms_rows(x, g):
    ms = jnp.mean(x * x, axis=-1, keepdims=True)
    return x * lax.rsqrt(ms + EPS) * g


def _sigmoid(x):
    return 1.0 / (1.0 + jnp.exp(-x))


def _proj_kernel(*refs, tm, pos0, pos_mask, prompt, cols):
    if prompt:
        (x_ref, g_ref, wt_ref, wr_ref, qg_ref, kg_ref, inv_ref, cw_ref,
         qT_ref, kT_ref, vT_ref, y_ref, sga_ref, sgc_ref, u_ref,
         k3_ref, v3_ref, km_ref, carry_ref) = refs
    else:
        (x_ref, g_ref, wt_ref, wr_ref, qg_ref, kg_ref, inv_ref, cw_ref, s1_ref, s2_ref,
         qT_ref, kT_ref, vT_ref, y_ref, sga_ref, sgc_ref, u_ref) = refs
    i = pl.program_id(0)
    width = x_ref.shape[1]

    h = _rms_rows(x_ref[...], g_ref[...]).astype(BF16)

    zT = lax.dot_general(wt_ref[...], h, (((1,), (1,)), ((), ())), preferred_element_type=F32)

    col = lax.broadcasted_iota(jnp.int32, (1, tm), 1) + i * tm
    if pos_mask is not None:
        col = col & pos_mask
    ang = inv_ref[...] * (col + pos0).astype(F32)
    cos = jnp.cos(ang)
    sin = jnp.sin(ang)

    def norm_rope(zt, gain):
        outs = []
        for hh in range(N_HEADS):
            blk = zt[hh * HEAD_DIM:(hh + 1) * HEAD_DIM]
            ms = jnp.mean(blk * blk, axis=0, keepdims=True)
            blk = blk * lax.rsqrt(ms + EPS) * gain
            x1 = blk[0:ROT_HALF]
            x2 = blk[ROT_HALF:ROT_DIM]
            outs += [x1 * cos - x2 * sin, x2 * cos + x1 * sin, blk[ROT_DIM:HEAD_DIM]]
        return jnp.concatenate(outs, axis=0)

    qT = norm_rope(zT[0:ATTN_W], qg_ref[...])
    kT = norm_rope(zT[ATTN_W:2 * ATTN_W], kg_ref[...])
    vT = zT[2 * ATTN_W:3 * ATTN_W]
    qT_ref[...] = qT
    kT_ref[...] = kT
    vT_ref[...] = vT

    if prompt:
        k_rows = kT.T
        pad_row = lax.broadcasted_iota(jnp.int32, (V_ROWS - HEAD_DIM, MOBA_BLOCK), 0)
        ones_pad = jnp.where(pad_row == 0, 1.0, 0.0).astype(BF16)
        for g in range(tm // MOBA_BLOCK):
            kb = k_rows[g * MOBA_BLOCK:(g + 1) * MOBA_BLOCK]
            km_ref[0, g:g + 1, :] = jnp.mean(kb, axis=0, keepdims=True)
            k3_ref[g] = kb.astype(BF16)
            vb = vT[:, g * MOBA_BLOCK:(g + 1) * MOBA_BLOCK].astype(BF16)
            pieces = []
            for hh in range(N_HEADS):
                pieces += [vb[hh * HEAD_DIM:(hh + 1) * HEAD_DIM], ones_pad]
            v3_ref[g] = jnp.concatenate(pieces, axis=0)

    if prompt:
        @pl.when(i == 0)
        def _():
            carry_ref[...] = jnp.zeros_like(carry_ref)
    else:
        rm = lax.broadcasted_iota(jnp.int32, (tm, 1), 0) & pos_mask
    for c0 in range(0, width, cols):
        cs = slice(c0, c0 + cols)

        def seg(k, c0=c0):
            return jnp.dot(h, wr_ref[:, k * width + c0:k * width + c0 + cols], preferred_element_type=F32)

        u = seg(2) * seg(0)
        carry = carry_ref[:, cs] if prompt else jnp.zeros((V7X_SUBLANES, cols), F32)
        up = jnp.concatenate([carry, u], axis=0)
        u1 = up[V7X_SUBLANES - 1:V7X_SUBLANES - 1 + tm]
        u2 = up[V7X_SUBLANES - 2:V7X_SUBLANES - 2 + tm]
        if not prompt:
            u1 = jnp.where(rm >= 1, u1, s1_ref[:, cs])
            u2 = jnp.where(rm >= 2, u2, s2_ref[:, cs])
        conv = cw_ref[0:1, cs] * u2 + cw_ref[1:2, cs] * u1 + cw_ref[2:3, cs] * u
        y_ref[:, cs] = (seg(1) * conv).astype(BF16)
        sga_ref[:, cs] = _sigmoid(seg(3)).astype(BF16)
        sgc_ref[:, cs] = _sigmoid(seg(4)).astype(BF16)
        if prompt:
            tail = u[tm - V7X_SUBLANES:tm]
            carry_ref[:, cs] = tail
            u_ref[:, cs] = tail
        else:
            u_ref[:, cs] = u


def _project(x2d, g_mix, w_qkvT, w_rest, qg, kg, inv, cw, *, pos0, prev=None):
    n, d = x2d.shape
    cfg = _tiles(n)
    tm = cfg["tm"]
    prompt = prev is None
    nt = n // tm
    row = lambda w: pl.BlockSpec((tm, w), lambda i: (i, 0))
    colT = pl.BlockSpec((ATTN_W, tm), lambda i: (0, i))
    in_specs = [row(d), _resident(g_mix.shape), _resident(w_qkvT.shape), _resident(w_rest.shape),
                _resident(qg.shape), _resident(kg.shape), _resident(inv.shape), _resident(cw.shape)]
    args = [x2d, g_mix, w_qkvT, w_rest, qg, kg, inv, cw]
    out_shape = [jax.ShapeDtypeStruct((ATTN_W, n), F32)] * 3 + [jax.ShapeDtypeStruct((n, d), BF16)] * 3
    out_specs = [colT, colT, colT, row(d), row(d), row(d)]
    scratch = []
    if prompt:
        assert tm % MOBA_BLOCK == 0
        gpt = tm // MOBA_BLOCK
        out_shape += [jax.ShapeDtypeStruct((V7X_SUBLANES, d), F32),
                      jax.ShapeDtypeStruct((n // MOBA_BLOCK, MOBA_BLOCK, ATTN_W), BF16),
                      jax.ShapeDtypeStruct((n // MOBA_BLOCK, N_HEADS * V_ROWS, MOBA_BLOCK), BF16),
                      jax.ShapeDtypeStruct((nt, gpt, ATTN_W), F32)]
        out_specs += [pl.BlockSpec((V7X_SUBLANES, d), lambda i: (0, 0)),
                      pl.BlockSpec((gpt, MOBA_BLOCK, ATTN_W), lambda i: (i, 0, 0)),
                      pl.BlockSpec((gpt, N_HEADS * V_ROWS, MOBA_BLOCK), lambda i: (i, 0, 0)),
                      pl.BlockSpec((1, gpt, ATTN_W), lambda i: (i, 0, 0))]
        scratch = [pltpu.VMEM((V7X_SUBLANES, d), F32)]
        pos_mask = None
    else:
        s1, s2, seq = prev
        assert seq & (seq - 1) == 0
        pos_mask = seq - 1
        in_specs += [row(d), row(d)]
        args += [s1, s2]
        out_shape += [jax.ShapeDtypeStruct((n, d), F32)]
        out_specs += [row(d)]
    return pl.pallas_call(
        functools.partial(_proj_kernel, tm=tm, pos0=pos0, pos_mask=pos_mask, prompt=prompt,
                          cols=cfg["proj_cols"]),
        grid=(nt,), in_specs=in_specs, out_specs=out_specs, out_shape=out_shape, scratch_shapes=scratch,
        compiler_params=pltpu.CompilerParams(dimension_semantics=("arbitrary",),
                                             vmem_limit_bytes=cfg["proj_vmem"]),
        name="proj_prompt" if prompt else "proj_sample",
    )(*args)


def _attn_kernel(q_ref, km_ref, k_hbm, v_hbm, o_ref,
                 kbuf, vbuf, qm_ref, sel_ref, acc_ref, m_ref, sem, *, nb, q_split, look, unroll):
    i = pl.program_id(0)
    blk_q = MOBA_BLOCK

    @pl.when(i == 0)
    def _():
        ck = pltpu.make_async_copy(k_hbm, kbuf, sem.at[0])
        cv = pltpu.make_async_copy(v_hbm, vbuf, sem.at[1])
        ck.start()
        cv.start()
        ck.wait()
        cv.wait()

    pair_w = 2 * HEAD_DIM
    blk = lax.broadcasted_iota(jnp.int32, (nb, blk_q), 0)
    rid = lax.broadcasted_iota(jnp.int32, (pair_w, 1), 0)
    for h in range(N_HEADS):
        pr = h // 2
        qp = q_ref[pr * pair_w:(pr + 1) * pair_w, :]
        keep = (rid >= HEAD_DIM) if h % 2 else (rid < HEAD_DIM)
        qm = jnp.where(keep, qp, 0.0)
        gate = jnp.dot(km_ref[:, pr * pair_w:(pr + 1) * pair_w], qm,
                       precision=HIGHEST, preferred_element_type=F32)
        g = jnp.where(blk < i, gate, -jnp.inf)
        sel = jnp.zeros((nb, blk_q), F32)
        for r in range(MOBA_TOPK):
            mx = jnp.max(g, axis=0, keepdims=True)
            idx = jnp.min(jnp.where(g == mx, blk, nb), axis=0, keepdims=True)
            hit = blk == idx
            sel = jnp.where(jnp.logical_and(hit, i > r), 1.0, sel)
            g = jnp.where(hit, -jnp.inf, g)
        sel_ref[h] = sel
        qm_ref[h] = (qm * (SM_SCALE * LOG2E)).astype(BF16)

    qw = blk_q // q_split

    def scores(j, h, c):
        pr = h // 2
        return jnp.dot(kbuf[j, :, pr * pair_w:(pr + 1) * pair_w], qm_ref[h, :, c * qw:(c + 1) * qw],
                       preferred_element_type=F32)

    def pv(j, h, p):
        return jnp.dot(vbuf[j, h * V_ROWS:(h + 1) * V_ROWS, :], p.astype(BF16), preferred_element_type=F32)

    kq = lax.broadcasted_iota(jnp.int32, (blk_q, qw), 0)
    qq = lax.broadcasted_iota(jnp.int32, (blk_q, qw), 1)

    def run_tiles(js, own):
        tiles = [(j, h, c) for j in js for h in range(N_HEADS) for c in range(q_split)]
        pending = {}
        for k in range(len(tiles) + look):
            if k < len(tiles):
                pending[k] = scores(*tiles[k])
            if k < look:
                continue
            j, h, c = tiles[k - look]
            s = pending.pop(k - look)
            cols = slice(c * qw, (c + 1) * qw)
            rows = slice(h * V_ROWS, (h + 1) * V_ROWS)
            if own:
                s = jnp.where(kq <= qq + c * qw, s, NEG)
                m = jnp.max(s, axis=0, keepdims=True)
                m_ref[h:h + 1, cols] = m
                acc_ref[rows, cols] = pv(j, h, jnp.exp2(s - m))
            else:
                picked = sel_ref[h, pl.ds(j, 1), :][:, cols] > 0.0
                m_old = m_ref[h:h + 1, cols]
                m_new = jnp.where(picked, jnp.maximum(m_old, jnp.max(s, axis=0, keepdims=True)), m_old)
                m_ref[h:h + 1, cols] = m_new
                part = pv(j, h, jnp.exp2(s - jnp.where(picked, m_new, BIG)))
                acc_ref[rows, cols] = jnp.exp2(m_old - m_new) * acc_ref[rows, cols] + part

    run_tiles([i], own=True)

    def past_body(jj, c):
        run_tiles([unroll * jj + b for b in range(unroll)], own=False)
        return c

    lax.fori_loop(0, i // unroll, past_body, 0)
    if unroll > 1:
        def tail_body(j, c):
            run_tiles([j], own=False)
            return c

        lax.fori_loop((i // unroll) * unroll, i, tail_body, 0)

    outs = []
    for h in range(N_HEADS):
        a = acc_ref[h * V_ROWS:(h + 1) * V_ROWS, :]
        outs.append(a[0:HEAD_DIM] / a[HEAD_DIM:HEAD_DIM + 1])
    o_ref[...] = jnp.concatenate(outs, axis=0).T.astype(BF16)


def _prompt_attention(qT, kmeans, k3, v3):
    nb = k3.shape[0]
    t = nb * MOBA_BLOCK
    cfg = _tiles(t)
    return pl.pallas_call(
        functools.partial(_attn_kernel, nb=nb, q_split=cfg["attn_q_split"], look=cfg["attn_look"],
                          unroll=cfg["attn_unroll"]),
        grid=(nb,),
        in_specs=[pl.BlockSpec((ATTN_W, MOBA_BLOCK), lambda i: (0, i)),
                  _resident(kmeans.shape),
                  pl.BlockSpec(memory_space=pl.ANY),
                  pl.BlockSpec(memory_space=pl.ANY)],
        out_specs=pl.BlockSpec((MOBA_BLOCK, ATTN_W), lambda i: (i, 0)),
        out_shape=jax.ShapeDtypeStruct((t, ATTN_W), BF16),
        scratch_shapes=[pltpu.VMEM(k3.shape, BF16), pltpu.VMEM(v3.shape, BF16),
                        pltpu.VMEM((N_HEADS, 2 * HEAD_DIM, MOBA_BLOCK), BF16),
                        pltpu.VMEM((N_HEADS, nb, MOBA_BLOCK), F32),
                        pltpu.VMEM((N_HEADS * V_ROWS, MOBA_BLOCK), F32),
                        pltpu.VMEM((N_HEADS, MOBA_BLOCK), F32),
                        pltpu.SemaphoreType.DMA((2,))],
        compiler_params=pltpu.CompilerParams(dimension_semantics=("arbitrary",),
                                             vmem_limit_bytes=cfg["attn_vmem"]),
        name="moba_prompt",
    )(qT, kmeans, k3, v3)


def _dec_kernel(pt_ref, qg_ref, ksum_ref, qs_ref, kn_ref, vn_ref, ck_hbm, cv_hbm, o_ref,
                idxv_ref, idxs_ref, kg, vg, vlast, sem_g, sem_misc, *, n_pages, seq):
    b = pl.program_id(0)
    n_seq = pl.num_programs(0) - 1
    ppb = MOBA_BLOCK // V7X_LANES
    nblk = n_pages // ppb
    per_q = MOBA_TOPK * ppb
    n_tiles = seq * per_q

    def head_rows(fn, width):
        sub = lax.broadcasted_iota(jnp.int32, (N_HEADS, width), 0)
        out = jnp.broadcast_to(fn(0), (N_HEADS, width))
        for h in range(1, N_HEADS):
            out = jnp.where(sub == h, fn(h), out)
        return out

    def hs(h):
        return slice(h * HEAD_DIM, (h + 1) * HEAD_DIM)

    def tile_copies(st, t, h, page):
        return (pltpu.make_async_copy(ck_hbm.at[page, hs(h)], kg.at[st, t, hs(h)], sem_g.at[st, 0]),
                pltpu.make_async_copy(cv_hbm.at[page, hs(h)], vg.at[st, t, hs(h)], sem_g.at[st, 1]))

    def last_copy(st, page):
        return pltpu.make_async_copy(cv_hbm.at[page], vlast.at[st], sem_misc.at[1 + st])

    @pl.when(b < n_seq)
    def _():
        st = b % 2
        kmt = ksum_ref[0] * (1.0 / MOBA_BLOCK)
        lane8 = lax.broadcasted_iota(jnp.int32, (N_HEADS, V7X_LANES), 1)
        idxv = jnp.zeros((N_HEADS, V7X_LANES), jnp.int32)
        for s in range(seq):
            qcol = qg_ref[0, :, s:s + 1]
            g = head_rows(lambda h: jnp.sum(kmt[hs(h)] * qcol[hs(h)], axis=0, keepdims=True), V7X_LANES)
            g = jnp.where(lane8 < nblk, g, -jnp.inf)
            for r in range(MOBA_TOPK):
                mx = jnp.max(g, axis=1, keepdims=True)
                idx = jnp.min(jnp.where(g == mx, lane8, V7X_LANES), axis=1, keepdims=True)
                idxv = jnp.where(lane8 == s * MOBA_TOPK + r, idx, idxv)
                g = jnp.where(lane8 == idx, -jnp.inf, g)
        idxv_ref[...] = idxv
        cp = pltpu.make_async_copy(idxv_ref, idxs_ref, sem_misc.at[0])
        cp.start()
        cp.wait()

        def issue(t, c):
            sr = lax.div(t, ppb)
            pg = t - sr * ppb
            for h in range(N_HEADS):
                ck, cv = tile_copies(st, t, h, pt_ref[b, idxs_ref[h, sr] * ppb + pg])
                ck.start()
                cv.start()
            return c

        lax.fori_loop(0, n_tiles, issue, 0)
        last_copy(st, pt_ref[b, n_pages - 1]).start()

    @pl.when(b >= 1)
    def _():
        st = (b - 1) % 2
        for _ in range(n_tiles * N_HEADS):
            ck, cv = tile_copies(st, 0, 0, 0)
            ck.wait()
            cv.wait()
        last_copy(st, 0).wait()

        lane_s = lax.broadcasted_iota(jnp.int32, (N_HEADS, seq), 1)
        cols = []
        for s in range(seq):
            qcol = qs_ref[0, :, s:s + 1] * SM_SCALE

            def tile_scores(t, s=s, qcol=qcol):
                return head_rows(
                    lambda h: jnp.sum(kg[st, s * per_q + t, hs(h), :] * qcol[hs(h)], axis=0, keepdims=True),
                    V7X_LANES)

            scs = [tile_scores(t) for t in range(per_q)]
            loc = head_rows(lambda h: jnp.sum(kn_ref[0, hs(h), :] * qcol[hs(h)], axis=0, keepdims=True), seq)
            loc = jnp.where(lane_s <= s, loc, NEG)
            mt = scs[0]
            for sc in scs[1:]:
                mt = jnp.maximum(mt, sc)
            m = jnp.maximum(jnp.max(mt, axis=1, keepdims=True), jnp.max(loc, axis=1, keepdims=True))
            ps = [jnp.exp(sc - m) for sc in scs]
            ploc = jnp.exp(loc - m)
            plast = jnp.exp(NEG - m)
            psum = ps[0]
            for p_ in ps[1:]:
                psum = psum + p_
            l = jnp.sum(psum, axis=1, keepdims=True) + jnp.sum(ploc, axis=1, keepdims=True) + V7X_LANES * plast
            rl = 1.0 / l
            outs = []
            for h in range(N_HEADS):
                acc = vlast[st, hs(h), :] * plast[h:h + 1, :]
                for t in range(per_q):
                    acc = acc + vg[st, s * per_q + t, hs(h), :] * ps[t][h:h + 1, :]
                o = jnp.sum(acc, axis=1, keepdims=True)
                o = o + jnp.sum(vn_ref[0, hs(h), :] * ploc[h:h + 1, :], axis=1, keepdims=True)
                outs.append(o * rl[h:h + 1, :])
            cols.append(jnp.concatenate(outs, axis=0))
        o_ref[0] = jnp.concatenate(cols, axis=1)


def _decode_attention(page_table, qs, kn, vn, ksum, ckT, cvT):
    bsz, _, seq = qs.shape
    n_pages = page_table.shape[1]
    ppb = MOBA_BLOCK // V7X_LANES
    nblk = n_pages // ppb
    assert ckT.shape[1:] == (ATTN_W, V7X_LANES) and n_pages % ppb == 0
    assert nblk <= V7X_LANES and seq * MOBA_TOPK <= V7X_LANES and ksum.shape == (bsz, ATTN_W, V7X_LANES)
    n_tiles = seq * MOBA_TOPK * ppb
    cfg = _tiles(bsz * seq)
    sel = lambda w: pl.BlockSpec((1, ATTN_W, w), lambda b, pt: (jnp.minimum(b, bsz - 1), 0, 0))
    att = pl.BlockSpec((1, ATTN_W, seq), lambda b, pt: (jnp.maximum(b - 1, 0), 0, 0))
    grid_spec = pltpu.PrefetchScalarGridSpec(
        num_scalar_prefetch=1, grid=(bsz + 1,),
        in_specs=[sel(seq), sel(V7X_LANES), att, att, att,
                  pl.BlockSpec(memory_space=pl.ANY), pl.BlockSpec(memory_space=pl.ANY)],
        out_specs=att,
        scratch_shapes=[pltpu.VMEM((N_HEADS, V7X_LANES), jnp.int32),
                        pltpu.SMEM((N_HEADS, V7X_LANES), jnp.int32),
                        pltpu.VMEM((2, n_tiles, ATTN_W, V7X_LANES), F32),
                        pltpu.VMEM((2, n_tiles, ATTN_W, V7X_LANES), F32),
                        pltpu.VMEM((2, ATTN_W, V7X_LANES), F32),
                        pltpu.SemaphoreType.DMA((2, 2)),
                        pltpu.SemaphoreType.DMA((3,))])
    return pl.pallas_call(
        functools.partial(_dec_kernel, n_pages=n_pages, seq=seq),
        grid_spec=grid_spec,
        out_shape=jax.ShapeDtypeStruct((bsz, ATTN_W, seq), F32),
        compiler_params=pltpu.CompilerParams(dimension_semantics=("arbitrary",),
                                             vmem_limit_bytes=cfg["dec_vmem"]),
        name="moba_decode",
    )(page_table, qs, ksum, qs, kn, vn, ckT, cvT)


def _sc_block_sums(page_table, ckT):
    bsz, n_pages = page_table.shape
    n_pool = ckT.shape[0]
    ppb = MOBA_BLOCK // V7X_LANES
    nblk = n_pages // ppb
    n_workers = V7X_SC_CORES * V7X_SC_SUBCORES
    assert bsz % n_workers == 0 and ATTN_W % SC_PIECE_ROWS == 0
    seq_per_worker = bsz // n_workers
    pieces = ATTN_W // SC_PIECE_ROWS
    steps = nblk * pieces
    table = ckT.reshape(n_pool * pieces, SC_PIECE_ROWS, V7X_LANES)
    idx = (page_table.reshape(bsz, nblk, 1, ppb) * pieces
           + jnp.arange(pieces, dtype=jnp.int32).reshape(1, 1, pieces, 1)).reshape(-1)
    lanes_per_row = V7X_LANES // V7X_SC_LANES

    def body(idx_hbm, tab_hbm, out_hbm, idx_v, buf, orow, sem):
        wid = lax.axis_index("c") * V7X_SC_SUBCORES + lax.axis_index("s")
        lane = lax.iota(jnp.int32, V7X_SC_LANES)

        def fetch(n, slot):
            return pltpu.make_async_copy(tab_hbm.at[idx_v.at[pl.ds(n * ppb, ppb)]], buf.at[slot], sem.at[slot])

        for q in range(seq_per_worker):
            sq = wid * seq_per_worker + q
            pltpu.sync_copy(idx_hbm.at[pl.ds(sq * (steps * ppb), steps * ppb)], idx_v)
            fetch(0, 0).start()

            @pl.loop(0, steps, step=2)
            def _(n0):
                for slot in range(2):
                    n = n0 + slot

                    @pl.when(n + 1 < steps)
                    def _():
                        fetch(n + 1, 1 - slot).start()

                    fetch(n, slot).wait()

                    @pl.loop(0, SC_PIECE_ROWS, step=V7X_SC_LANES)
                    def _(r0):
                        vec = jnp.zeros((V7X_SC_LANES,), F32)
                        for i in range(V7X_SC_LANES):
                            acc = None
                            for pg in range(ppb):
                                for k in range(lanes_per_row):
                                    v = buf[slot, pg, r0 + i, pl.ds(k * V7X_SC_LANES, V7X_SC_LANES)]
                                    acc = v if acc is None else acc + v
                            vec = jnp.where(lane == i, jnp.sum(acc), vec)
                        orow[pl.ds(r0, V7X_SC_LANES)] = vec

                    pltpu.sync_copy(orow, out_hbm.at[pl.ds((sq * steps + n) * SC_PIECE_ROWS, SC_PIECE_ROWS)])

    sums = pl.kernel(
        body,
        out_type=jax.ShapeDtypeStruct((bsz * steps * SC_PIECE_ROWS,), F32),
        mesh=plsc.VectorSubcoreMesh(core_axis_name="c", subcore_axis_name="s"),
        scratch_types=[pltpu.VMEM((steps * ppb,), jnp.int32),
                       pltpu.VMEM((2, ppb, SC_PIECE_ROWS, V7X_LANES), F32),
                       pltpu.VMEM((SC_PIECE_ROWS,), F32),
                       pltpu.SemaphoreType.DMA((2,))],
        compiler_params=pltpu.CompilerParams(needs_layout_passes=False),
        name="sc_block_sums",
    )(idx, table)
    ksum = jnp.transpose(sums.reshape(bsz, nblk, ATTN_W), (0, 2, 1))
    return jnp.pad(ksum, ((0, 0), (0, 0), (0, V7X_LANES - nblk)))


def _mlp_kernel(x_ref, a_ref, y_ref, sga_ref, sgc_ref, wba_ref, wbc_ref, wo_ref, g_ref, wup_ref, wdn_ref,
                o_ref, *, chunk):
    a = jnp.dot(a_ref[...], wba_ref[...], preferred_element_type=F32)
    c = jnp.dot(y_ref[...], wbc_ref[...], preferred_element_type=F32)
    mixed = sga_ref[...].astype(F32) * a + sgc_ref[...].astype(F32) * c
    x1 = x_ref[...] + jnp.dot(mixed.astype(BF16), wo_ref[...], preferred_element_type=F32)
    h2 = _rms_rows(x1, g_ref[...]).astype(BF16)
    acc = x1
    for c0 in range(0, wup_ref.shape[1], chunk):
        hid = jnp.maximum(jnp.dot(h2, wup_ref[:, c0:c0 + chunk], preferred_element_type=F32), 0.0)
        acc = acc + jnp.dot((hid * hid).astype(BF16), wdn_ref[c0:c0 + chunk, :], preferred_element_type=F32)
    o_ref[...] = acc


def _merge_mlp(x2d, attn, y, sga, sgc, wba, wbc, wo, g_mlp, wup, wdn):
    n, d = x2d.shape
    cfg = _tiles(n)
    tm = cfg["tm"]
    row = lambda w: pl.BlockSpec((tm, w), lambda i: (i, 0))
    weights = [wba, wbc, wo, g_mlp, wup, wdn]
    return pl.pallas_call(
        functools.partial(_mlp_kernel, chunk=d),
        grid=(n // tm,),
        in_specs=[row(d), row(attn.shape[1]), row(d), row(d), row(d)] + [_resident(w.shape) for w in weights],
        out_specs=row(d),
        out_shape=jax.ShapeDtypeStruct((n, d), F32),
        compiler_params=pltpu.CompilerParams(dimension_semantics=("arbitrary",),
                                             vmem_limit_bytes=cfg["mlp_vmem"]),
        name="merge_mlp",
    )(x2d, attn, y, sga, sgc, *weights)


def _heads_last(xT, lead):
    n = xT.shape[1]
    return jnp.transpose(xT.reshape(N_HEADS, HEAD_DIM, n), (2, 0, 1)).reshape(*lead, N_HEADS, HEAD_DIM)


def kernel(x_prompt, x_sample, cache_k, cache_v, state_conv, page_table, norm_mix, w_in, q_norm, k_norm, conv_w, w_br_attn, w_br_conv, w_out, norm_mlp, w_up, w_down):
    bp, t, d = x_prompt.shape
    db, s, _ = x_sample.shape
    depth = w_in.shape[0]
    n_pool, page = cache_k.shape[1], cache_k.shape[2]
    assert bp == 1 and page == V7X_LANES and t % MOBA_BLOCK == 0
    past = page_table.shape[1] * page
    inv = (ROPE_THETA ** (-jnp.arange(ROT_HALF, dtype=F32) * 2.0 / ROT_DIM)).reshape(ROT_HALF, 1)

    ckT_l = [jnp.transpose(cache_k[l], (0, 2, 3, 1)).reshape(n_pool, ATTN_W, page) for l in range(depth)]
    cvT_l = [jnp.transpose(cache_v[l], (0, 2, 3, 1)).reshape(n_pool, ATTN_W, page) for l in range(depth)]
    ksum = [_sc_block_sums(page_table, ckT_l[l]) for l in range(depth)]

    xp = x_prompt.reshape(t, d)
    xs = x_sample.reshape(db * s, d)
    kp_l, vp_l, cp_l, ks_l, vs_l, cs_l = [], [], [], [], [], []
    for l in range(depth):
        w_qkvT = w_in[l][:, :3 * ATTN_W].T.astype(BF16)
        w_rest = w_in[l][:, 3 * ATTN_W:].astype(BF16)
        g_mix = norm_mix[l].reshape(1, d)
        qg = q_norm[l].reshape(HEAD_DIM, 1)
        kg = k_norm[l].reshape(HEAD_DIM, 1)
        cw = conv_w[l]
        weights = (w_br_attn[l].astype(BF16), w_br_conv[l].astype(BF16), w_out[l].astype(BF16),
                   norm_mlp[l].reshape(1, d), w_up[l].astype(BF16), w_down[l].astype(BF16))

        qT, kT, vT, y, sga, sgc, u_tail, k3, v3, km = _project(xp, g_mix, w_qkvT, w_rest, qg, kg, inv, cw, pos0=0)
        attn = _prompt_attention(qT, km.reshape(t // MOBA_BLOCK, ATTN_W), k3, v3)
        xp = _merge_mlp(xp, attn, y, sga, sgc, *weights)
        kp_l.append(_heads_last(kT, (bp, t)))
        vp_l.append(_heads_last(vT, (bp, t)))
        cp_l.append(u_tail[V7X_SUBLANES - (CONV_K - 1):].reshape(bp, CONV_K - 1, d))

        st = state_conv[l]
        zero = jnp.zeros((db, s, d), F32)
        s1 = zero.at[:, 0].set(st[:, 1]).reshape(db * s, d)
        s2 = zero.at[:, 0].set(st[:, 0]).at[:, 1].set(st[:, 1]).reshape(db * s, d)
        qT, kT, vT, y, sga, sgc, u = _project(xs, g_mix, w_qkvT, w_rest, qg, kg, inv, cw, pos0=past,
                                               prev=(s1, s2, s))
        per_seq = lambda xT: jnp.transpose(xT.reshape(ATTN_W, db, s), (1, 0, 2))
        oT = _decode_attention(page_table, per_seq(qT), per_seq(kT), per_seq(vT), ksum[l], ckT_l[l], cvT_l[l])
        attn = jnp.transpose(oT, (0, 2, 1)).reshape(db * s, ATTN_W).astype(BF16)
        xs = _merge_mlp(xs, attn, y, sga, sgc, *weights)
        ks_l.append(_heads_last(kT, (db, s)))
        vs_l.append(_heads_last(vT, (db, s)))
        cs_l.append(u.reshape(db, s, d)[:, s - (CONV_K - 1):])

    return (xp.reshape(bp, t, d), xs.reshape(db, s, d), jnp.stack(kp_l), jnp.stack(vp_l), jnp.stack(cp_l),
            jnp.stack(ks_l), jnp.stack(vs_l), jnp.stack(cs_l))
```

```python
import functools

import jax
import jax.numpy as jnp
from jax import lax
from jax.experimental import pallas as pl
from jax.experimental.pallas import tpu as pltpu
from jax.experimental.pallas import tpu_sc as plsc

N_HEADS = 8
HEAD_DIM = 64
ATTN_W = N_HEADS * HEAD_DIM
MOBA_BLOCK = 256
MOBA_TOPK = 3
ROT_DIM = HEAD_DIM // 4
ROT_HALF = ROT_DIM // 2
ROPE_THETA = 500000.0
EPS = 1e-6
NEG = -1e30
CONV_K = 3
SM_SCALE = HEAD_DIM ** -0.5
LOG2E = 1.4426950408889634
BIG = 1e30
V_ROWS = HEAD_DIM + 16

V7X_VMEM_BYTES = 64 * 1024 * 1024
V7X_SUBLANES = 8
V7X_LANES = 128
MIB = 1024 * 1024
V7X_SC_CORES = 2
V7X_SC_SUBCORES = 16
V7X_SC_LANES = 16
SC_PIECE_ROWS = 128

F32 = jnp.float32
BF16 = jnp.bfloat16
HIGHEST = lax.Precision.HIGHEST


def _tiles(n_rows):
    tm = 512 if n_rows % 512 == 0 else n_rows
    return dict(tm=tm, proj_vmem=56 * MIB, attn_vmem=52 * MIB, mlp_vmem=56 * MIB, dec_vmem=56 * MIB,
                attn_q_split=1, attn_look=6, attn_unroll=(8, 4, 1), proj_cols=256, mlp_chunk=1024)


def _resident(shape):
    nd = len(shape)
    return pl.BlockSpec(shape, lambda *_: (0,) * nd, pipeline_mode=pl.Buffered(1))


def _rms_rows(x, g):
    ms = jnp.mean(x * x, axis=-1, keepdims=True)
    return x * lax.rsqrt(ms + EPS) * g


def _sigmoid(x):
    return 1.0 / (1.0 + jnp.exp(-x))


def _proj_kernel(*refs, tm, pos0, pos_mask, prompt, cols):
    if prompt:
        (x_ref, g_ref, wt_ref, wr_ref, qg_ref, kg_ref, inv_ref, cw_ref,
         qT_ref, kT_ref, vT_ref, y_ref, sga_ref, sgc_ref, u_ref,
         k3_ref, v3_ref, km_ref, carry_ref) = refs
    else:
        (x_ref, g_ref, wt_ref, wr_ref, qg_ref, kg_ref, inv_ref, cw_ref, s1_ref, s2_ref,
         qT_ref, kT_ref, vT_ref, y_ref, sga_ref, sgc_ref, u_ref) = refs
    i = pl.program_id(0)
    width = x_ref.shape[1]

    h = _rms_rows(x_ref[...], g_ref[...]).astype(BF16)

    zT = lax.dot_general(wt_ref[...], h, (((1,), (1,)), ((), ())), preferred_element_type=F32)

    col = lax.broadcasted_iota(jnp.int32, (1, tm), 1) + i * tm
    if pos_mask is not None:
        col = col & pos_mask
    ang = inv_ref[...] * (col + pos0).astype(F32)
    cos = jnp.cos(ang)
    sin = jnp.sin(ang)

    def norm_rope(zt, gain):
        outs = []
        for hh in range(N_HEADS):
            blk = zt[hh * HEAD_DIM:(hh + 1) * HEAD_DIM]
            ms = jnp.mean(blk * blk, axis=0, keepdims=True)
            blk = blk * lax.rsqrt(ms + EPS) * gain
            x1 = blk[0:ROT_HALF]
            x2 = blk[ROT_HALF:ROT_DIM]
            outs += [x1 * cos - x2 * sin, x2 * cos + x1 * sin, blk[ROT_DIM:HEAD_DIM]]
        return jnp.concatenate(outs, axis=0)

    qT = norm_rope(zT[0:ATTN_W], qg_ref[...])
    kT = norm_rope(zT[ATTN_W:2 * ATTN_W], kg_ref[...])
    vT = zT[2 * ATTN_W:3 * ATTN_W]
    qT_ref[...] = qT
    kT_ref[...] = kT
    vT_ref[...] = vT

    if prompt:
        k_rows = kT.T
        pad_row = lax.broadcasted_iota(jnp.int32, (V_ROWS - HEAD_DIM, MOBA_BLOCK), 0)
        ones_pad = jnp.where(pad_row == 0, 1.0, 0.0).astype(BF16)
        for g in range(tm // MOBA_BLOCK):
            kb = k_rows[g * MOBA_BLOCK:(g + 1) * MOBA_BLOCK]
            km_ref[0, g:g + 1, :] = jnp.mean(kb, axis=0, keepdims=True)
            k3_ref[g] = kb.astype(BF16)
            vb = vT[:, g * MOBA_BLOCK:(g + 1) * MOBA_BLOCK].astype(BF16)
            pieces = []
            for hh in range(N_HEADS):
                pieces += [vb[hh * HEAD_DIM:(hh + 1) * HEAD_DIM], ones_pad]
            v3_ref[g] = jnp.concatenate(pieces, axis=0)

    if prompt:
        @pl.when(i == 0)
        def _():
            carry_ref[...] = jnp.zeros_like(carry_ref)
    else:
        rm = lax.broadcasted_iota(jnp.int32, (tm, 1), 0) & pos_mask
    for c0 in range(0, width, cols):
        cs = slice(c0, c0 + cols)

        def seg(k, c0=c0):
            return jnp.dot(h, wr_ref[:, k * width + c0:k * width + c0 + cols], preferred_element_type=F32)

        u = seg(2) * seg(0)
        carry = carry_ref[:, cs] if prompt else jnp.zeros((V7X_SUBLANES, cols), F32)
        up = jnp.concatenate([carry, u], axis=0)
        u1 = up[V7X_SUBLANES - 1:V7X_SUBLANES - 1 + tm]
        u2 = up[V7X_SUBLANES - 2:V7X_SUBLANES - 2 + tm]
        if not prompt:
            u1 = jnp.where(rm >= 1, u1, s1_ref[:, cs])
            u2 = jnp.where(rm >= 2, u2, s2_ref[:, cs])
        conv = cw_ref[0:1, cs] * u2 + cw_ref[1:2, cs] * u1 + cw_ref[2:3, cs] * u
        y_ref[:, cs] = (seg(1) * conv).astype(BF16)
        sga_ref[:, cs] = _sigmoid(seg(3)).astype(BF16)
        sgc_ref[:, cs] = _sigmoid(seg(4)).astype(BF16)
        if prompt:
            tail = u[tm - V7X_SUBLANES:tm]
            carry_ref[:, cs] = tail
            u_ref[:, cs] = tail
        else:
            u_ref[:, cs] = u


def _project(x2d, g_mix, w_qkvT, w_rest, qg, kg, inv, cw, *, pos0, prev=None):
    n, d = x2d.shape
    cfg = _tiles(n)
    tm = cfg["tm"]
    prompt = prev is None
    nt = n // tm
    row = lambda w: pl.BlockSpec((tm, w), lambda i: (i, 0))
    colT = pl.BlockSpec((ATTN_W, tm), lambda i: (0, i))
    in_specs = [row(d), _resident(g_mix.shape), _resident(w_qkvT.shape), _resident(w_rest.shape),
                _resident(qg.shape), _resident(kg.shape), _resident(inv.shape), _resident(cw.shape)]
    args = [x2d, g_mix, w_qkvT, w_rest, qg, kg, inv, cw]
    out_shape = [jax.ShapeDtypeStruct((ATTN_W, n), F32)] * 3 + [jax.ShapeDtypeStruct((n, d), BF16)] * 3
    out_specs = [colT, colT, colT, row(d), row(d), row(d)]
    scratch = []
    if prompt:
        assert tm % MOBA_BLOCK == 0
        gpt = tm // MOBA_BLOCK
        out_shape += [jax.ShapeDtypeStruct((V7X_SUBLANES, d), F32),
                      jax.ShapeDtypeStruct((n // MOBA_BLOCK, MOBA_BLOCK, ATTN_W), BF16),
                      jax.ShapeDtypeStruct((n // MOBA_BLOCK, N_HEADS * V_ROWS, MOBA_BLOCK), BF16),
                      jax.ShapeDtypeStruct((nt, gpt, ATTN_W), F32)]
        out_specs += [pl.BlockSpec((V7X_SUBLANES, d), lambda i: (0, 0)),
                      pl.BlockSpec((gpt, MOBA_BLOCK, ATTN_W), lambda i: (i, 0, 0)),
                      pl.BlockSpec((gpt, N_HEADS * V_ROWS, MOBA_BLOCK), lambda i: (i, 0, 0)),
                      pl.BlockSpec((1, gpt, ATTN_W), lambda i: (i, 0, 0))]
        scratch = [pltpu.VMEM((V7X_SUBLANES, d), F32)]
        pos_mask = None
    else:
        s1, s2, seq = prev
        assert seq & (seq - 1) == 0
        pos_mask = seq - 1
        in_specs += [row(d), row(d)]
        args += [s1, s2]
        out_shape += [jax.ShapeDtypeStruct((n, d), F32)]
        out_specs += [row(d)]
    return pl.pallas_call(
        functools.partial(_proj_kernel, tm=tm, pos0=pos0, pos_mask=pos_mask, prompt=prompt,
                          cols=cfg["proj_cols"]),
        grid=(nt,), in_specs=in_specs, out_specs=out_specs, out_shape=out_shape, scratch_shapes=scratch,
        compiler_params=pltpu.CompilerParams(dimension_semantics=("arbitrary",),
                                             vmem_limit_bytes=cfg["proj_vmem"]),
        name="proj_prompt" if prompt else "proj_sample",
    )(*args)


def _attn_kernel(q_ref, km_ref, k_hbm, v_hbm, o_ref,
                 kbuf, vbuf, qm_ref, sel_ref, acc_ref, m_ref, sem, *, nb, q_split, look, unroll):
    i = pl.program_id(0)
    blk_q = MOBA_BLOCK

    @pl.when(i == 0)
    def _():
        ck = pltpu.make_async_copy(k_hbm, kbuf, sem.at[0])
        cv = pltpu.make_async_copy(v_hbm, vbuf, sem.at[1])
        ck.start()
        cv.start()
        ck.wait()
        cv.wait()

    pair_w = 2 * HEAD_DIM
    blk = lax.broadcasted_iota(jnp.int32, (nb, blk_q), 0)
    rid = lax.broadcasted_iota(jnp.int32, (pair_w, 1), 0)
    for h in range(N_HEADS):
        pr = h // 2
        qp = q_ref[pr * pair_w:(pr + 1) * pair_w, :]
        keep = (rid >= HEAD_DIM) if h % 2 else (rid < HEAD_DIM)
        qm = jnp.where(keep, qp, 0.0)
        gate = jnp.dot(km_ref[:, pr * pair_w:(pr + 1) * pair_w], qm,
                       precision=HIGHEST, preferred_element_type=F32)
        g = jnp.where(blk < i, gate, -jnp.inf)
        sel = jnp.zeros((nb, blk_q), F32)
        for r in range(MOBA_TOPK):
            mx = jnp.max(g, axis=0, keepdims=True)
            idx = jnp.min(jnp.where(g == mx, blk, nb), axis=0, keepdims=True)
            hit = blk == idx
            sel = jnp.where(jnp.logical_and(hit, i > r), 1.0, sel)
            g = jnp.where(hit, -jnp.inf, g)
        sel_ref[h] = sel
        qm_ref[h] = (qm * (SM_SCALE * LOG2E)).astype(BF16)

    qw = blk_q // q_split

    def scores(j, h, c):
        pr = h // 2
        return jnp.dot(kbuf[j, :, pr * pair_w:(pr + 1) * pair_w], qm_ref[h, :, c * qw:(c + 1) * qw],
                       preferred_element_type=F32)

    def pv(j, h, p):
        return jnp.dot(vbuf[j, h * V_ROWS:(h + 1) * V_ROWS, :], p.astype(BF16), preferred_element_type=F32)

    kq = lax.broadcasted_iota(jnp.int32, (blk_q, qw), 0)
    qq = lax.broadcasted_iota(jnp.int32, (blk_q, qw), 1)

    def run_tiles(js, own):
        tiles = [(j, h, c) for j in js for h in range(N_HEADS) for c in range(q_split)]
        pending = {}
        for k in range(len(tiles) + look):
            if k < len(tiles):
                pending[k] = scores(*tiles[k])
            if k < look:
                continue
            j, h, c = tiles[k - look]
            s = pending.pop(k - look)
            cols = slice(c * qw, (c + 1) * qw)
            rows = slice(h * V_ROWS, (h + 1) * V_ROWS)
            if own:
                s = jnp.where(kq <= qq + c * qw, s, NEG)
                m = jnp.max(s, axis=0, keepdims=True)
                m_ref[h:h + 1, cols] = m
                acc_ref[rows, cols] = pv(j, h, jnp.exp2(s - m))
            else:
                picked = sel_ref[h, pl.ds(j, 1), :][:, cols] > 0.0
                m_old = m_ref[h:h + 1, cols]
                m_new = jnp.where(picked, jnp.maximum(m_old, jnp.max(s, axis=0, keepdims=True)), m_old)
                m_ref[h:h + 1, cols] = m_new
                part = pv(j, h, jnp.exp2(s - jnp.where(picked, m_new, BIG)))
                acc_ref[rows, cols] = jnp.exp2(m_old - m_new) * acc_ref[rows, cols] + part

    run_tiles([i], own=True)

    done = 0
    for u in unroll:
        trips = (i - done) // u

        def past_body(jj, c, u=u, done=done):
            run_tiles([done + u * jj + b for b in range(u)], own=False)
            return c

        lax.fori_loop(0, trips, past_body, 0)
        done = done + trips * u

    outs = []
    for h in range(N_HEADS):
        a = acc_ref[h * V_ROWS:(h + 1) * V_ROWS, :]
        outs.append(a[0:HEAD_DIM] / a[HEAD_DIM:HEAD_DIM + 1])
    o_ref[...] = jnp.concatenate(outs, axis=0).T.astype(BF16)


def _prompt_attention(qT, kmeans, k3, v3):
    nb = k3.shape[0]
    t = nb * MOBA_BLOCK
    cfg = _tiles(t)
    return pl.pallas_call(
        functools.partial(_attn_kernel, nb=nb, q_split=cfg["attn_q_split"], look=cfg["attn_look"],
                          unroll=cfg["attn_unroll"]),
        grid=(nb,),
        in_specs=[pl.BlockSpec((ATTN_W, MOBA_BLOCK), lambda i: (0, i)),
                  _resident(kmeans.shape),
                  pl.BlockSpec(memory_space=pl.ANY),
                  pl.BlockSpec(memory_space=pl.ANY)],
        out_specs=pl.BlockSpec((MOBA_BLOCK, ATTN_W), lambda i: (i, 0)),
        out_shape=jax.ShapeDtypeStruct((t, ATTN_W), BF16),
        scratch_shapes=[pltpu.VMEM(k3.shape, BF16), pltpu.VMEM(v3.shape, BF16),
                        pltpu.VMEM((N_HEADS, 2 * HEAD_DIM, MOBA_BLOCK), BF16),
                        pltpu.VMEM((N_HEADS, nb, MOBA_BLOCK), F32),
                        pltpu.VMEM((N_HEADS * V_ROWS, MOBA_BLOCK), F32),
                        pltpu.VMEM((N_HEADS, MOBA_BLOCK), F32),
                        pltpu.SemaphoreType.DMA((2,))],
        compiler_params=pltpu.CompilerParams(dimension_semantics=("arbitrary",),
                                             vmem_limit_bytes=cfg["attn_vmem"]),
        name="moba_prompt",
    )(qT, kmeans, k3, v3)


def _dec_kernel(pt_ref, qg_ref, ksum_ref, qs_ref, kn_ref, vn_ref, ck_hbm, cv_hbm, o_ref,
                idxv_ref, idxs_ref, kg, vg, vlast, sem_g, sem_misc, *, n_pages, seq):
    b = pl.program_id(0)
    n_seq = pl.num_programs(0) - 1
    ppb = MOBA_BLOCK // V7X_LANES
    nblk = n_pages // ppb
    per_q = MOBA_TOPK * ppb
    n_tiles = seq * per_q

    def head_rows(fn, width):
        sub = lax.broadcasted_iota(jnp.int32, (N_HEADS, width), 0)
        out = jnp.broadcast_to(fn(0), (N_HEADS, width))
        for h in range(1, N_HEADS):
            out = jnp.where(sub == h, fn(h), out)
        return out

    def hs(h):
        return slice(h * HEAD_DIM, (h + 1) * HEAD_DIM)

    def tile_copies(st, t, h, page):
        return (pltpu.make_async_copy(ck_hbm.at[page, hs(h)], kg.at[st, t, hs(h)], sem_g.at[st, 0]),
                pltpu.make_async_copy(cv_hbm.at[page, hs(h)], vg.at[st, t, hs(h)], sem_g.at[st, 1]))

    def last_copy(st, page):
        return pltpu.make_async_copy(cv_hbm.at[page], vlast.at[st], sem_misc.at[1 + st])

    @pl.when(b < n_seq)
    def _():
        st = b % 2
        kmt = ksum_ref[0] * (1.0 / MOBA_BLOCK)
        lane8 = lax.broadcasted_iota(jnp.int32, (N_HEADS, V7X_LANES), 1)
        idxv = jnp.zeros((N_HEADS, V7X_LANES), jnp.int32)
        for s in range(seq):
            qcol = qg_ref[0, :, s:s + 1]
            g = head_rows(lambda h: jnp.sum(kmt[hs(h)] * qcol[hs(h)], axis=0, keepdims=True), V7X_LANES)
            g = jnp.where(lane8 < nblk, g, -jnp.inf)
            for r in range(MOBA_TOPK):
                mx = jnp.max(g, axis=1, keepdims=True)
                idx = jnp.min(jnp.where(g == mx, lane8, V7X_LANES), axis=1, keepdims=True)
                idxv = jnp.where(lane8 == s * MOBA_TOPK + r, idx, idxv)
                g = jnp.where(lane8 == idx, -jnp.inf, g)
        idxv_ref[...] = idxv
        cp = pltpu.make_async_copy(idxv_ref, idxs_ref, sem_misc.at[0])
        cp.start()
        cp.wait()

        def issue(t, c):
            sr = lax.div(t, ppb)
            pg = t - sr * ppb
            for h in range(N_HEADS):
                ck, cv = tile_copies(st, t, h, pt_ref[b, idxs_ref[h, sr] * ppb + pg])
                ck.start()
                cv.start()
            return c

        lax.fori_loop(0, n_tiles, issue, 0)
        last_copy(st, pt_ref[b, n_pages - 1]).start()

    @pl.when(b >= 1)
    def _():
        st = (b - 1) % 2
        for _ in range(n_tiles * N_HEADS):
            ck, cv = tile_copies(st, 0, 0, 0)
            ck.wait()
            cv.wait()
        last_copy(st, 0).wait()

        lane_s = lax.broadcasted_iota(jnp.int32, (N_HEADS, seq), 1)
        cols = []
        for s in range(seq):
            qcol = qs_ref[0, :, s:s + 1] * SM_SCALE

            def tile_scores(t, s=s, qcol=qcol):
                return head_rows(
                    lambda h: jnp.sum(kg[st, s * per_q + t, hs(h), :] * qcol[hs(h)], axis=0, keepdims=True),
                    V7X_LANES)

            scs = [tile_scores(t) for t in range(per_q)]
            loc = head_rows(lambda h: jnp.sum(kn_ref[0, hs(h), :] * qcol[hs(h)], axis=0, keepdims=True), seq)
            loc = jnp.where(lane_s <= s, loc, NEG)
            mt = scs[0]
            for sc in scs[1:]:
                mt = jnp.maximum(mt, sc)
            m = jnp.maximum(jnp.max(mt, axis=1, keepdims=True), jnp.max(loc, axis=1, keepdims=True))
            ps = [jnp.exp(sc - m) for sc in scs]
            ploc = jnp.exp(loc - m)
            plast = jnp.exp(NEG - m)
            psum = ps[0]
            for p_ in ps[1:]:
                psum = psum + p_
            l = jnp.sum(psum, axis=1, keepdims=True) + jnp.sum(ploc, axis=1, keepdims=True) + V7X_LANES * plast
            rl = 1.0 / l
            outs = []
            for h in range(N_HEADS):
                acc = vlast[st, hs(h), :] * plast[h:h + 1, :]
                for t in range(per_q):
                    acc = acc + vg[st, s * per_q + t, hs(h), :] * ps[t][h:h + 1, :]
                o = jnp.sum(acc, axis=1, keepdims=True)
                o = o + jnp.sum(vn_ref[0, hs(h), :] * ploc[h:h + 1, :], axis=1, keepdims=True)
                outs.append(o * rl[h:h + 1, :])
            cols.append(jnp.concatenate(outs, axis=0))
        o_ref[0] = jnp.concatenate(cols, axis=1)


def _decode_attention(page_table, qs, kn, vn, ksum, ckT, cvT):
    bsz, _, seq = qs.shape
    n_pages = page_table.shape[1]
    ppb = MOBA_BLOCK // V7X_LANES
    nblk = n_pages // ppb
    assert ckT.shape[1:] == (ATTN_W, V7X_LANES) and n_pages % ppb == 0
    assert nblk <= V7X_LANES and seq * MOBA_TOPK <= V7X_LANES and ksum.shape == (bsz, ATTN_W, V7X_LANES)
    n_tiles = seq * MOBA_TOPK * ppb
    cfg = _tiles(bsz * seq)
    sel = lambda w: pl.BlockSpec((1, ATTN_W, w), lambda b, pt: (jnp.minimum(b, bsz - 1), 0, 0))
    att = pl.BlockSpec((1, ATTN_W, seq), lambda b, pt: (jnp.maximum(b - 1, 0), 0, 0))
    grid_spec = pltpu.PrefetchScalarGridSpec(
        num_scalar_prefetch=1, grid=(bsz + 1,),
        in_specs=[sel(seq), sel(V7X_LANES), att, att, att,
                  pl.BlockSpec(memory_space=pl.ANY), pl.BlockSpec(memory_space=pl.ANY)],
        out_specs=att,
        scratch_shapes=[pltpu.VMEM((N_HEADS, V7X_LANES), jnp.int32),
                        pltpu.SMEM((N_HEADS, V7X_LANES), jnp.int32),
                        pltpu.VMEM((2, n_tiles, ATTN_W, V7X_LANES), F32),
                        pltpu.VMEM((2, n_tiles, ATTN_W, V7X_LANES), F32),
                        pltpu.VMEM((2, ATTN_W, V7X_LANES), F32),
                        pltpu.SemaphoreType.DMA((2, 2)),
                        pltpu.SemaphoreType.DMA((3,))])
    return pl.pallas_call(
        functools.partial(_dec_kernel, n_pages=n_pages, seq=seq),
        grid_spec=grid_spec,
        out_shape=jax.ShapeDtypeStruct((bsz, ATTN_W, seq), F32),
        compiler_params=pltpu.CompilerParams(dimension_semantics=("arbitrary",),
                                             vmem_limit_bytes=cfg["dec_vmem"]),
        name="moba_decode",
    )(page_table, qs, ksum, qs, kn, vn, ckT, cvT)


def _sc_block_sums(page_table, ckT):
    bsz, n_pages = page_table.shape
    n_pool = ckT.shape[0]
    ppb = MOBA_BLOCK // V7X_LANES
    nblk = n_pages // ppb
    n_workers = V7X_SC_CORES * V7X_SC_SUBCORES
    assert bsz % n_workers == 0 and ATTN_W % SC_PIECE_ROWS == 0
    seq_per_worker = bsz // n_workers
    pieces = ATTN_W // SC_PIECE_ROWS
    steps = nblk * pieces
    table = ckT.reshape(n_pool * pieces, SC_PIECE_ROWS, V7X_LANES)
    idx = (page_table.reshape(bsz, nblk, 1, ppb) * pieces
           + jnp.arange(pieces, dtype=jnp.int32).reshape(1, 1, pieces, 1)).reshape(-1)
    lanes_per_row = V7X_LANES // V7X_SC_LANES

    def body(idx_hbm, tab_hbm, out_hbm, idx_v, buf, orow, sem):
        wid = lax.axis_index("c") * V7X_SC_SUBCORES + lax.axis_index("s")
        lane = lax.iota(jnp.int32, V7X_SC_LANES)

        def fetch(n, slot):
            return pltpu.make_async_copy(tab_hbm.at[idx_v.at[pl.ds(n * ppb, ppb)]], buf.at[slot], sem.at[slot])

        for q in range(seq_per_worker):
            sq = wid * seq_per_worker + q
            pltpu.sync_copy(idx_hbm.at[pl.ds(sq * (steps * ppb), steps * ppb)], idx_v)
            fetch(0, 0).start()

            @pl.loop(0, steps, step=2)
            def _(n0):
                for slot in range(2):
                    n = n0 + slot

                    @pl.when(n + 1 < steps)
                    def _():
                        fetch(n + 1, 1 - slot).start()

                    fetch(n, slot).wait()

                    @pl.loop(0, SC_PIECE_ROWS, step=V7X_SC_LANES)
                    def _(r0):
                        vec = jnp.zeros((V7X_SC_LANES,), F32)
                        for i in range(V7X_SC_LANES):
                            acc = None
                            for pg in range(ppb):
                                for k in range(lanes_per_row):
                                    v = buf[slot, pg, r0 + i, pl.ds(k * V7X_SC_LANES, V7X_SC_LANES)]
                                    acc = v if acc is None else acc + v
                            vec = jnp.where(lane == i, jnp.sum(acc), vec)
                        orow[pl.ds(r0, V7X_SC_LANES)] = vec

                    pltpu.sync_copy(orow, out_hbm.at[pl.ds((sq * steps + n) * SC_PIECE_ROWS, SC_PIECE_ROWS)])

    sums = pl.kernel(
        body,
        out_type=jax.ShapeDtypeStruct((bsz * steps * SC_PIECE_ROWS,), F32),
        mesh=plsc.VectorSubcoreMesh(core_axis_name="c", subcore_axis_name="s"),
        scratch_types=[pltpu.VMEM((steps * ppb,), jnp.int32),
                       pltpu.VMEM((2, ppb, SC_PIECE_ROWS, V7X_LANES), F32),
                       pltpu.VMEM((SC_PIECE_ROWS,), F32),
                       pltpu.SemaphoreType.DMA((2,))],
        compiler_params=pltpu.CompilerParams(needs_layout_passes=False),
        name="sc_block_sums",
    )(idx, table)
    ksum = jnp.transpose(sums.reshape(bsz, nblk, ATTN_W), (0, 2, 1))
    return jnp.pad(ksum, ((0, 0), (0, 0), (0, V7X_LANES - nblk)))


def _mlp_kernel(x_ref, a_ref, y_ref, sga_ref, sgc_ref, wba_ref, wbc_ref, wo_ref, g_ref, wup_ref, wdn_ref,
                o_ref, *, chunk):
    a = jnp.dot(a_ref[...], wba_ref[...], preferred_element_type=F32)
    c = jnp.dot(y_ref[...], wbc_ref[...], preferred_element_type=F32)
    mixed = sga_ref[...].astype(F32) * a + sgc_ref[...].astype(F32) * c
    x1 = x_ref[...] + jnp.dot(mixed.astype(BF16), wo_ref[...], preferred_element_type=F32)
    h2 = _rms_rows(x1, g_ref[...]).astype(BF16)
    acc = x1
    for c0 in range(0, wup_ref.shape[1], chunk):
        hid = jnp.maximum(jnp.dot(h2, wup_ref[:, c0:c0 + chunk], preferred_element_type=F32), 0.0)
        acc = acc + jnp.dot((hid * hid).astype(BF16), wdn_ref[c0:c0 + chunk, :], preferred_element_type=F32)
    o_ref[...] = acc


def _merge_mlp(x2d, attn, y, sga, sgc, wba, wbc, wo, g_mlp, wup, wdn):
    n, d = x2d.shape
    cfg = _tiles(n)
    tm = cfg["tm"]
    row = lambda w: pl.BlockSpec((tm, w), lambda i: (i, 0))
    weights = [wba, wbc, wo, g_mlp, wup, wdn]
    return pl.pallas_call(
        functools.partial(_mlp_kernel, chunk=cfg["mlp_chunk"]),
        grid=(n // tm,),
        in_specs=[row(d), row(attn.shape[1]), row(d), row(d), row(d)] + [_resident(w.shape) for w in weights],
        out_specs=row(d),
        out_shape=jax.ShapeDtypeStruct((n, d), F32),
        compiler_params=pltpu.CompilerParams(dimension_semantics=("arbitrary",),
                                             vmem_limit_bytes=cfg["mlp_vmem"]),
        name="merge_mlp",
    )(x2d, attn, y, sga, sgc, *weights)


def _heads_last(xT, lead):
    n = xT.shape[1]
    return jnp.transpose(xT.reshape(N_HEADS, HEAD_DIM, n), (2, 0, 1)).reshape(*lead, N_HEADS, HEAD_DIM)


def kernel(x_prompt, x_sample, cache_k, cache_v, state_conv, page_table, norm_mix, w_in, q_norm, k_norm, conv_w, w_br_attn, w_br_conv, w_out, norm_mlp, w_up, w_down):
    bp, t, d = x_prompt.shape
    db, s, _ = x_sample.shape
    depth = w_in.shape[0]
    n_pool, page = cache_k.shape[1], cache_k.shape[2]
    assert bp == 1 and page == V7X_LANES and t % MOBA_BLOCK == 0
    past = page_table.shape[1] * page
    inv = (ROPE_THETA ** (-jnp.arange(ROT_HALF, dtype=F32) * 2.0 / ROT_DIM)).reshape(ROT_HALF, 1)

    ckT_l = [jnp.transpose(cache_k[l], (0, 2, 3, 1)).reshape(n_pool, ATTN_W, page) for l in range(depth)]
    cvT_l = [jnp.transpose(cache_v[l], (0, 2, 3, 1)).reshape(n_pool, ATTN_W, page) for l in range(depth)]
    ksum = [_sc_block_sums(page_table, ckT_l[l]) for l in range(depth)]

    xp = x_prompt.reshape(t, d)
    xs = x_sample.reshape(db * s, d)
    kp_l, vp_l, cp_l, ks_l, vs_l, cs_l = [], [], [], [], [], []
    for l in range(depth):
        w_qkvT = w_in[l][:, :3 * ATTN_W].T.astype(BF16)
        w_rest = w_in[l][:, 3 * ATTN_W:].astype(BF16)
        g_mix = norm_mix[l].reshape(1, d)
        qg = q_norm[l].reshape(HEAD_DIM, 1)
        kg = k_norm[l].reshape(HEAD_DIM, 1)
        cw = conv_w[l]
        weights = (w_br_attn[l].astype(BF16), w_br_conv[l].astype(BF16), w_out[l].astype(BF16),
                   norm_mlp[l].reshape(1, d), w_up[l].astype(BF16), w_down[l].astype(BF16))

        qT, kT, vT, y, sga, sgc, u_tail, k3, v3, km = _project(xp, g_mix, w_qkvT, w_rest, qg, kg, inv, cw, pos0=0)
        attn = _prompt_attention(qT, km.reshape(t // MOBA_BLOCK, ATTN_W), k3, v3)
        xp = _merge_mlp(xp, attn, y, sga, sgc, *weights)
        kp_l.append(_heads_last(kT, (bp, t)))
        vp_l.append(_heads_last(vT, (bp, t)))
        cp_l.append(u_tail[V7X_SUBLANES - (CONV_K - 1):].reshape(bp, CONV_K - 1, d))

        st = state_conv[l]
        zero = jnp.zeros((db, s, d), F32)
        s1 = zero.at[:, 0].set(st[:, 1]).reshape(db * s, d)
        s2 = zero.at[:, 0].set(st[:, 0]).at[:, 1].set(st[:, 1]).reshape(db * s, d)
        qT, kT, vT, y, sga, sgc, u = _project(xs, g_mix, w_qkvT, w_rest, qg, kg, inv, cw, pos0=past,
                                               prev=(s1, s2, s))
        per_seq = lambda xT: jnp.transpose(xT.reshape(ATTN_W, db, s), (1, 0, 2))
        oT = _decode_attention(page_table, per_seq(qT), per_seq(kT), per_seq(vT), ksum[l], ckT_l[l], cvT_l[l])
        attn = jnp.transpose(oT, (0, 2, 1)).reshape(db * s, ATTN_W).astype(BF16)
        xs = _merge_mlp(xs, attn, y, sga, sgc, *weights)
        ks_l.append(_heads_last(kT, (db, s)))
        vs_l.append(_heads_last(vT, (db, s)))
        cs_l.append(u.reshape(db, s, d)[:, s - (CONV_K - 1):])

    return (xp.reshape(bp, t, d), xs.reshape(db, s, d), jnp.stack(kp_l), jnp.stack(vp_l), jnp.stack(cp_l),
            jnp.stack(ks_l), jnp.stack(vs_l), jnp.stack(cs_l))
```

```python
import functools

import jax
import jax.numpy as jnp
from jax import lax
from jax.experimental import pallas as pl
from jax.experimental.pallas import tpu as pltpu
from jax.experimental.pallas import tpu_sc as plsc

N_HEADS = 8
HEAD_DIM = 64
ATTN_W = N_HEADS * HEAD_DIM
MOBA_BLOCK = 256
MOBA_TOPK = 3
ROT_DIM = HEAD_DIM // 4
ROT_HALF = ROT_DIM // 2
ROPE_THETA = 500000.0
EPS = 1e-6
NEG = -1e30
CONV_K = 3
SM_SCALE = HEAD_DIM ** -0.5
LOG2E = 1.4426950408889634
BIG = 1e30
V_ROWS = HEAD_DIM + 16

V7X_VMEM_BYTES = 64 * 1024 * 1024
V7X_SUBLANES = 8
V7X_LANES = 128
MIB = 1024 * 1024
V7X_SC_CORES = 2
V7X_SC_SUBCORES = 16
V7X_SC_LANES = 16
SC_PIECE_ROWS = 128

F32 = jnp.float32
BF16 = jnp.bfloat16
HIGHEST = lax.Precision.HIGHEST


def _tiles(n_rows):
    tm = 512 if n_rows % 512 == 0 else n_rows
    return dict(tm=tm, proj_vmem=56 * MIB, attn_vmem=52 * MIB, mlp_vmem=56 * MIB, dec_vmem=56 * MIB,
                attn_q_split=1, attn_look=6, attn_unroll=(8, 4, 1), proj_cols=256, mlp_chunk=1024)


def _resident(shape):
    nd = len(shape)
    return pl.BlockSpec(shape, lambda *_: (0,) * nd, pipeline_mode=pl.Buffered(1))


def _rms_rows(x, g):
    ms = jnp.mean(x * x, axis=-1, keepdims=True)
    return x * lax.rsqrt(ms + EPS) * g


def _sigmoid(x):
    return 1.0 / (1.0 + jnp.exp(-x))


def _proj_kernel(*refs, tm, pos0, pos_mask, prompt, cols):
    if prompt:
        (x_ref, g_ref, wt_ref, wr_ref, qg_ref, kg_ref, inv_ref, cw_ref,
         qT_ref, kT_ref, vT_ref, y_ref, sga_ref, sgc_ref, u_ref,
         k3_ref, v3_ref, km_ref, carry_ref) = refs
    else:
        (x_ref, g_ref, wt_ref, wr_ref, qg_ref, kg_ref, inv_ref, cw_ref, s1_ref, s2_ref,
         qT_ref, kT_ref, vT_ref, y_ref, sga_ref, sgc_ref, u_ref) = refs
    i = pl.program_id(0)
    width = x_ref.shape[1]

    h = _rms_rows(x_ref[...], g_ref[...]).astype(BF16)

    zT = lax.dot_general(wt_ref[...], h, (((1,), (1,)), ((), ())), preferred_element_type=F32)

    col = lax.broadcasted_iota(jnp.int32, (1, tm), 1) + i * tm
    if pos_mask is not None:
        col = col & pos_mask
    ang = inv_ref[...] * (col + pos0).astype(F32)
    cos = jnp.cos(ang)
    sin = jnp.sin(ang)

    def norm_rope(zt, gain):
        outs = []
        for hh in range(N_HEADS):
            blk = zt[hh * HEAD_DIM:(hh + 1) * HEAD_DIM]
            ms = jnp.mean(blk * blk, axis=0, keepdims=True)
            blk = blk * lax.rsqrt(ms + EPS) * gain
            x1 = blk[0:ROT_HALF]
            x2 = blk[ROT_HALF:ROT_DIM]
            outs += [x1 * cos - x2 * sin, x2 * cos + x1 * sin, blk[ROT_DIM:HEAD_DIM]]
        return jnp.concatenate(outs, axis=0)

    qT = norm_rope(zT[0:ATTN_W], qg_ref[...])
    kT = norm_rope(zT[ATTN_W:2 * ATTN_W], kg_ref[...])
    vT = zT[2 * ATTN_W:3 * ATTN_W]
    qT_ref[...] = qT
    kT_ref[...] = kT
    vT_ref[...] = vT

    if prompt:
        k_rows = kT.T
        pad_row = lax.broadcasted_iota(jnp.int32, (V_ROWS - HEAD_DIM, MOBA_BLOCK), 0)
        ones_pad = jnp.where(pad_row == 0, 1.0, 0.0).astype(BF16)
        for g in range(tm // MOBA_BLOCK):
            kb = k_rows[g * MOBA_BLOCK:(g + 1) * MOBA_BLOCK]
            km_ref[0, g:g + 1, :] = jnp.mean(kb, axis=0, keepdims=True)
            k3_ref[g] = kb.astype(BF16)
            vb = vT[:, g * MOBA_BLOCK:(g + 1) * MOBA_BLOCK].astype(BF16)
            pieces = []
            for hh in range(N_HEADS):
                pieces += [vb[hh * HEAD_DIM:(hh + 1) * HEAD_DIM], ones_pad]
            v3_ref[g] = jnp.concatenate(pieces, axis=0)

    if prompt:
        @pl.when(i == 0)
        def _():
            carry_ref[...] = jnp.zeros_like(carry_ref)
    else:
        rm = lax.broadcasted_iota(jnp.int32, (tm, 1), 0) & pos_mask
    for c0 in range(0, width, cols):
        cs = slice(c0, c0 + cols)

        def seg(k, c0=c0):
            return jnp.dot(h, wr_ref[:, k * width + c0:k * width + c0 + cols], preferred_element_type=F32)

        u = seg(2) * seg(0)
        carry = carry_ref[:, cs] if prompt else jnp.zeros((V7X_SUBLANES, cols), F32)
        up = jnp.concatenate([carry, u], axis=0)
        u1 = up[V7X_SUBLANES - 1:V7X_SUBLANES - 1 + tm]
        u2 = up[V7X_SUBLANES - 2:V7X_SUBLANES - 2 + tm]
        if not prompt:
            u1 = jnp.where(rm >= 1, u1, s1_ref[:, cs])
            u2 = jnp.where(rm >= 2, u2, s2_ref[:, cs])
        conv = cw_ref[0:1, cs] * u2 + cw_ref[1:2, cs] * u1 + cw_ref[2:3, cs] * u
        y_ref[:, cs] = (seg(1) * conv).astype(BF16)
        sga_ref[:, cs] = _sigmoid(seg(3)).astype(BF16)
        sgc_ref[:, cs] = _sigmoid(seg(4)).astype(BF16)
        if prompt:
            tail = u[tm - V7X_SUBLANES:tm]
            carry_ref[:, cs] = tail
            u_ref[:, cs] = tail
        else:
            u_ref[:, cs] = u


def _project(x2d, g_mix, w_qkvT, w_rest, qg, kg, inv, cw, *, pos0, prev=None):
    n, d = x2d.shape
    cfg = _tiles(n)
    tm = cfg["tm"]
    prompt = prev is None
    nt = n // tm
    row = lambda w: pl.BlockSpec((tm, w), lambda i: (i, 0))
    colT = pl.BlockSpec((ATTN_W, tm), lambda i: (0, i))
    in_specs = [row(d), _resident(g_mix.shape), _resident(w_qkvT.shape), _resident(w_rest.shape),
                _resident(qg.shape), _resident(kg.shape), _resident(inv.shape), _resident(cw.shape)]
    args = [x2d, g_mix, w_qkvT, w_rest, qg, kg, inv, cw]
    out_shape = [jax.ShapeDtypeStruct((ATTN_W, n), F32)] * 3 + [jax.ShapeDtypeStruct((n, d), BF16)] * 3
    out_specs = [colT, colT, colT, row(d), row(d), row(d)]
    scratch = []
    if prompt:
        assert tm % MOBA_BLOCK == 0
        gpt = tm // MOBA_BLOCK
        out_shape += [jax.ShapeDtypeStruct((V7X_SUBLANES, d), F32),
                      jax.ShapeDtypeStruct((n // MOBA_BLOCK, MOBA_BLOCK, ATTN_W), BF16),
                      jax.ShapeDtypeStruct((n // MOBA_BLOCK, N_HEADS * V_ROWS, MOBA_BLOCK), BF16),
                      jax.ShapeDtypeStruct((nt, gpt, ATTN_W), F32)]
        out_specs += [pl.BlockSpec((V7X_SUBLANES, d), lambda i: (0, 0)),
                      pl.BlockSpec((gpt, MOBA_BLOCK, ATTN_W), lambda i: (i, 0, 0)),
                      pl.BlockSpec((gpt, N_HEADS * V_ROWS, MOBA_BLOCK), lambda i: (i, 0, 0)),
                      pl.BlockSpec((1, gpt, ATTN_W), lambda i: (i, 0, 0))]
        scratch = [pltpu.VMEM((V7X_SUBLANES, d), F32)]
        pos_mask = None
    else:
        s1, s2, seq = prev
        assert seq & (seq - 1) == 0
        pos_mask = seq - 1
        in_specs += [row(d), row(d)]
        args += [s1, s2]
        out_shape += [jax.ShapeDtypeStruct((n, d), F32)]
        out_specs += [row(d)]
    return pl.pallas_call(
        functools.partial(_proj_kernel, tm=tm, pos0=pos0, pos_mask=pos_mask, prompt=prompt,
                          cols=cfg["proj_cols"]),
        grid=(nt,), in_specs=in_specs, out_specs=out_specs, out_shape=out_shape, scratch_shapes=scratch,
        compiler_params=pltpu.CompilerParams(dimension_semantics=("arbitrary",),
                                             vmem_limit_bytes=cfg["proj_vmem"]),
        name="proj_prompt" if prompt else "proj_sample",
    )(*args)


def _attn_kernel(q_ref, km_ref, k_hbm, v_hbm, o_ref,
                 kbuf, vbuf, qm_ref, sel_ref, acc_ref, m_ref, sem, *, nb, q_split, look, unroll):
    i = pl.program_id(0)
    blk_q = MOBA_BLOCK

    def block_copies(j):
        return (pltpu.make_async_copy(k_hbm.at[j], kbuf.at[j], sem.at[j, 0]),
                pltpu.make_async_copy(v_hbm.at[j], vbuf.at[j], sem.at[j, 1]))

    @pl.when(i == 0)
    def _():
        def start(j, c):
            for cp in block_copies(j):
                cp.start()
            return c
        lax.fori_loop(0, nb, start, 0)

    for cp in block_copies(i):
        cp.wait()

    pair_w = 2 * HEAD_DIM
    blk = lax.broadcasted_iota(jnp.int32, (nb, blk_q), 0)
    rid = lax.broadcasted_iota(jnp.int32, (pair_w, 1), 0)
    for h in range(N_HEADS):
        pr = h // 2
        qp = q_ref[pr * pair_w:(pr + 1) * pair_w, :]
        keep = (rid >= HEAD_DIM) if h % 2 else (rid < HEAD_DIM)
        qm = jnp.where(keep, qp, 0.0)
        gate = jnp.dot(km_ref[:, pr * pair_w:(pr + 1) * pair_w], qm,
                       precision=HIGHEST, preferred_element_type=F32)
        g = jnp.where(blk < i, gate, -jnp.inf)
        sel = jnp.zeros((nb, blk_q), F32)
        for r in range(MOBA_TOPK):
            mx = jnp.max(g, axis=0, keepdims=True)
            idx = jnp.min(jnp.where(g == mx, blk, nb), axis=0, keepdims=True)
            hit = blk == idx
            sel = jnp.where(jnp.logical_and(hit, i > r), 1.0, sel)
            g = jnp.where(hit, -jnp.inf, g)
        sel_ref[h] = sel
        qm_ref[h] = (qm * (SM_SCALE * LOG2E)).astype(BF16)

    qw = blk_q // q_split

    def scores(j, h, c):
        pr = h // 2
        return jnp.dot(kbuf[j, :, pr * pair_w:(pr + 1) * pair_w], qm_ref[h, :, c * qw:(c + 1) * qw],
                       preferred_element_type=F32)

    def pv(j, h, p):
        return jnp.dot(vbuf[j, h * V_ROWS:(h + 1) * V_ROWS, :], p.astype(BF16), preferred_element_type=F32)

    kq = lax.broadcasted_iota(jnp.int32, (blk_q, qw), 0)
    qq = lax.broadcasted_iota(jnp.int32, (blk_q, qw), 1)

    def run_tiles(js, own):
        tiles = [(j, h, c) for j in js for h in range(N_HEADS) for c in range(q_split)]
        pending = {}
        for k in range(len(tiles) + look):
            if k < len(tiles):
                pending[k] = scores(*tiles[k])
            if k < look:
                continue
            j, h, c = tiles[k - look]
            s = pending.pop(k - look)
            cols = slice(c * qw, (c + 1) * qw)
            rows = slice(h * V_ROWS, (h + 1) * V_ROWS)
            if own:
                s = jnp.where(kq <= qq + c * qw, s, NEG)
                m = jnp.max(s, axis=0, keepdims=True)
                m_ref[h:h + 1, cols] = m
                acc_ref[rows, cols] = pv(j, h, jnp.exp2(s - m))
            else:
                picked = sel_ref[h, pl.ds(j, 1), :][:, cols] > 0.0
                m_old = m_ref[h:h + 1, cols]
                m_new = jnp.where(picked, jnp.maximum(m_old, jnp.max(s, axis=0, keepdims=True)), m_old)
                m_ref[h:h + 1, cols] = m_new
                part = pv(j, h, jnp.exp2(s - jnp.where(picked, m_new, BIG)))
                acc_ref[rows, cols] = jnp.exp2(m_old - m_new) * acc_ref[rows, cols] + part

    run_tiles([i], own=True)

    done = 0
    for u in unroll:
        trips = (i - done) // u

        def past_body(jj, c, u=u, done=done):
            run_tiles([done + u * jj + b for b in range(u)], own=False)
            return c

        lax.fori_loop(0, trips, past_body, 0)
        done = done + trips * u

    outs = []
    for h in range(N_HEADS):
        a = acc_ref[h * V_ROWS:(h + 1) * V_ROWS, :]
        outs.append(a[0:HEAD_DIM] / a[HEAD_DIM:HEAD_DIM + 1])
    o_ref[...] = jnp.concatenate(outs, axis=0).T.astype(BF16)


def _prompt_attention(qT, kmeans, k3, v3):
    nb = k3.shape[0]
    t = nb * MOBA_BLOCK
    cfg = _tiles(t)
    return pl.pallas_call(
        functools.partial(_attn_kernel, nb=nb, q_split=cfg["attn_q_split"], look=cfg["attn_look"],
                          unroll=cfg["attn_unroll"]),
        grid=(nb,),
        in_specs=[pl.BlockSpec((ATTN_W, MOBA_BLOCK), lambda i: (0, i)),
                  _resident(kmeans.shape),
                  pl.BlockSpec(memory_space=pl.ANY),
                  pl.BlockSpec(memory_space=pl.ANY)],
        out_specs=pl.BlockSpec((MOBA_BLOCK, ATTN_W), lambda i: (i, 0)),
        out_shape=jax.ShapeDtypeStruct((t, ATTN_W), BF16),
        scratch_shapes=[pltpu.VMEM(k3.shape, BF16), pltpu.VMEM(v3.shape, BF16),
                        pltpu.VMEM((N_HEADS, 2 * HEAD_DIM, MOBA_BLOCK), BF16),
                        pltpu.VMEM((N_HEADS, nb, MOBA_BLOCK), F32),
                        pltpu.VMEM((N_HEADS * V_ROWS, MOBA_BLOCK), F32),
                        pltpu.VMEM((N_HEADS, MOBA_BLOCK), F32),
                        pltpu.SemaphoreType.DMA((nb, 2))],
        compiler_params=pltpu.CompilerParams(dimension_semantics=("arbitrary",),
                                             vmem_limit_bytes=cfg["attn_vmem"]),
        name="moba_prompt",
    )(qT, kmeans, k3, v3)


def _dec_kernel(pt_ref, qbd_ref, ksum_ref, qkv_ref, ck_hbm, cv_hbm, o_ref,
                idxv_ref, idxs_ref, kg, vg, vlast, sem_g, sem_misc, *, n_pages, seq):
    b = pl.program_id(0)
    n_seq = pl.num_programs(0) - 1
    ppb = MOBA_BLOCK // V7X_LANES
    nblk = n_pages // ppb
    per_q = MOBA_TOPK * ppb
    n_tiles = seq * per_q

    def head_rows(fn, width):
        sub = lax.broadcasted_iota(jnp.int32, (N_HEADS, width), 0)
        out = jnp.broadcast_to(fn(0), (N_HEADS, width))
        for h in range(1, N_HEADS):
            out = jnp.where(sub == h, fn(h), out)
        return out

    def hs(h):
        return slice(h * HEAD_DIM, (h + 1) * HEAD_DIM)

    def tile_copies(st, t, h, page):
        return (pltpu.make_async_copy(ck_hbm.at[page, hs(h)], kg.at[st, t, hs(h)], sem_g.at[st, 0]),
                pltpu.make_async_copy(cv_hbm.at[page, hs(h)], vg.at[st, t, hs(h)], sem_g.at[st, 1]))

    def last_copy(st, page):
        return pltpu.make_async_copy(cv_hbm.at[page], vlast.at[st], sem_misc.at[1 + st])

    @pl.when(b < n_seq)
    def _():
        st = b % 2
        g = jnp.dot(ksum_ref[0], qbd_ref[0], precision=HIGHEST, preferred_element_type=F32)
        g = g * (1.0 / MOBA_BLOCK)
        blk = lax.broadcasted_iota(jnp.int32, g.shape, 0)
        sub8 = lax.broadcasted_iota(jnp.int32, (V7X_SUBLANES, V7X_LANES), 0)
        idxv = jnp.zeros((V7X_SUBLANES, V7X_LANES), jnp.int32)
        for r in range(MOBA_TOPK):
            mx = jnp.max(g, axis=0, keepdims=True)
            idx = jnp.min(jnp.where(g == mx, blk, nblk), axis=0, keepdims=True)
            idxv = jnp.where(sub8 == r, idx, idxv)
            g = jnp.where(blk == idx, -jnp.inf, g)
        idxv_ref[...] = idxv
        cp = pltpu.make_async_copy(idxv_ref, idxs_ref, sem_misc.at[0])
        cp.start()
        cp.wait()

        def issue(t, c):
            sr = lax.div(t, ppb)
            pg = t - sr * ppb
            s = lax.div(sr, MOBA_TOPK)
            r = sr - s * MOBA_TOPK
            for h in range(N_HEADS):
                ck, cv = tile_copies(st, t, h, pt_ref[b, idxs_ref[r, h * seq + s] * ppb + pg])
                ck.start()
                cv.start()
            return c

        lax.fori_loop(0, n_tiles, issue, 0)
        last_copy(st, pt_ref[b, n_pages - 1]).start()

    @pl.when(b >= 1)
    def _():
        st = (b - 1) % 2
        for _ in range(n_tiles * N_HEADS):
            ck, cv = tile_copies(st, 0, 0, 0)
            ck.wait()
            cv.wait()
        last_copy(st, 0).wait()

        lane_s = lax.broadcasted_iota(jnp.int32, (N_HEADS, seq), 1)
        cols = []
        for s in range(seq):
            qcol = qkv_ref[0, 0, :, s:s + 1] * SM_SCALE

            def tile_scores(t, s=s, qcol=qcol):
                return head_rows(
                    lambda h: jnp.sum(kg[st, s * per_q + t, hs(h), :] * qcol[hs(h)], axis=0, keepdims=True),
                    V7X_LANES)

            scs = [tile_scores(t) for t in range(per_q)]
            loc = head_rows(lambda h: jnp.sum(qkv_ref[0, 1, hs(h), :] * qcol[hs(h)], axis=0, keepdims=True), seq)
            loc = jnp.where(lane_s <= s, loc, NEG)
            mt = scs[0]
            for sc in scs[1:]:
                mt = jnp.maximum(mt, sc)
            m = jnp.maximum(jnp.max(mt, axis=1, keepdims=True), jnp.max(loc, axis=1, keepdims=True))
            ps = [jnp.exp(sc - m) for sc in scs]
            ploc = jnp.exp(loc - m)
            plast = jnp.exp(NEG - m)
            psum = ps[0]
            for p_ in ps[1:]:
                psum = psum + p_
            l = jnp.sum(psum, axis=1, keepdims=True) + jnp.sum(ploc, axis=1, keepdims=True) + V7X_LANES * plast
            rl = 1.0 / l
            outs = []
            for h in range(N_HEADS):
                acc = vlast[st, hs(h), :] * plast[h:h + 1, :]
                for t in range(per_q):
                    acc = acc + vg[st, s * per_q + t, hs(h), :] * ps[t][h:h + 1, :]
                o = jnp.sum(acc, axis=1, keepdims=True)
                o = o + jnp.sum(qkv_ref[0, 2, hs(h), :] * ploc[h:h + 1, :], axis=1, keepdims=True)
                outs.append(o * rl[h:h + 1, :])
            cols.append(jnp.concatenate(outs, axis=0))
        o_ref[0] = jnp.concatenate(cols, axis=1)


def _decode_attention(page_table, qkv, ksum, ckT, cvT):
    bsz, _, _, seq = qkv.shape
    n_pages = page_table.shape[1]
    ppb = MOBA_BLOCK // V7X_LANES
    nblk = n_pages // ppb
    assert ckT.shape[1:] == (ATTN_W, V7X_LANES) and n_pages % ppb == 0 and ksum.shape == (bsz, nblk, ATTN_W)
    assert MOBA_TOPK <= V7X_SUBLANES and N_HEADS * seq <= V7X_LANES
    n_tiles = seq * MOBA_TOPK * ppb
    cfg = _tiles(bsz * seq)
    head_of_row = jnp.arange(ATTN_W, dtype=jnp.int32)[:, None] // HEAD_DIM
    head_of_col = jnp.arange(N_HEADS * seq, dtype=jnp.int32)[None, :] // seq
    qbd = jnp.where(head_of_row == head_of_col, jnp.tile(qkv[:, 0], (1, 1, N_HEADS)), 0.0)
    qbd = jnp.pad(qbd, ((0, 0), (0, 0), (0, V7X_LANES - N_HEADS * seq)))
    sel = lambda r, w: pl.BlockSpec((1, r, w), lambda b, pt: (jnp.minimum(b, bsz - 1), 0, 0))
    att = pl.BlockSpec((1, ATTN_W, seq), lambda b, pt: (jnp.maximum(b - 1, 0), 0, 0))
    grid_spec = pltpu.PrefetchScalarGridSpec(
        num_scalar_prefetch=1, grid=(bsz + 1,),
        in_specs=[sel(ATTN_W, V7X_LANES), sel(nblk, ATTN_W),
                  pl.BlockSpec((1, 3, ATTN_W, seq), lambda b, pt: (jnp.maximum(b - 1, 0), 0, 0, 0)),
                  pl.BlockSpec(memory_space=pl.ANY), pl.BlockSpec(memory_space=pl.ANY)],
        out_specs=att,
        scratch_shapes=[pltpu.VMEM((N_HEADS, V7X_LANES), jnp.int32),
                        pltpu.SMEM((N_HEADS, V7X_LANES), jnp.int32),
                        pltpu.VMEM((2, n_tiles, ATTN_W, V7X_LANES), F32),
                        pltpu.VMEM((2, n_tiles, ATTN_W, V7X_LANES), F32),
                        pltpu.VMEM((2, ATTN_W, V7X_LANES), F32),
                        pltpu.SemaphoreType.DMA((2, 2)),
                        pltpu.SemaphoreType.DMA((3,))])
    return pl.pallas_call(
        functools.partial(_dec_kernel, n_pages=n_pages, seq=seq),
        grid_spec=grid_spec,
        out_shape=jax.ShapeDtypeStruct((bsz, ATTN_W, seq), F32),
        compiler_params=pltpu.CompilerParams(dimension_semantics=("arbitrary",),
                                             vmem_limit_bytes=cfg["dec_vmem"]),
        name="moba_decode",
    )(page_table, qbd, ksum, qkv, ckT, cvT)


def _sc_block_sums(page_table, ckT):
    bsz, n_pages = page_table.shape
    n_pool = ckT.shape[0]
    ppb = MOBA_BLOCK // V7X_LANES
    nblk = n_pages // ppb
    n_workers = V7X_SC_CORES * V7X_SC_SUBCORES
    assert bsz % n_workers == 0 and ATTN_W % SC_PIECE_ROWS == 0
    seq_per_worker = bsz // n_workers
    pieces = ATTN_W // SC_PIECE_ROWS
    steps = nblk * pieces
    table = ckT.reshape(n_pool * pieces, SC_PIECE_ROWS, V7X_LANES)
    idx = (page_table.reshape(bsz, nblk, 1, ppb) * pieces
           + jnp.arange(pieces, dtype=jnp.int32).reshape(1, 1, pieces, 1)).reshape(-1)
    lanes_per_row = V7X_LANES // V7X_SC_LANES

    def body(idx_hbm, tab_hbm, out_hbm, idx_v, buf, orow, sem):
        wid = lax.axis_index("c") * V7X_SC_SUBCORES + lax.axis_index("s")
        lane = lax.iota(jnp.int32, V7X_SC_LANES)

        def fetch(n, slot):
            return pltpu.make_async_copy(tab_hbm.at[idx_v.at[pl.ds(n * ppb, ppb)]], buf.at[slot], sem.at[slot])

        for q in range(seq_per_worker):
            sq = wid * seq_per_worker + q
            pltpu.sync_copy(idx_hbm.at[pl.ds(sq * (steps * ppb), steps * ppb)], idx_v)
            fetch(0, 0).start()

            @pl.loop(0, steps, step=2)
            def _(n0):
                for slot in range(2):
                    n = n0 + slot

                    @pl.when(n + 1 < steps)
                    def _():
                        fetch(n + 1, 1 - slot).start()

                    fetch(n, slot).wait()

                    @pl.loop(0, SC_PIECE_ROWS, step=V7X_SC_LANES)
                    def _(r0):
                        vec = jnp.zeros((V7X_SC_LANES,), F32)
                        for i in range(V7X_SC_LANES):
                            acc = None
                            for pg in range(ppb):
                                for k in range(lanes_per_row):
                                    v = buf[slot, pg, r0 + i, pl.ds(k * V7X_SC_LANES, V7X_SC_LANES)]
                                    acc = v if acc is None else acc + v
                            vec = jnp.where(lane == i, jnp.sum(acc), vec)
                        orow[pl.ds(r0, V7X_SC_LANES)] = vec

                    pltpu.sync_copy(orow, out_hbm.at[pl.ds((sq * steps + n) * SC_PIECE_ROWS, SC_PIECE_ROWS)])

    sums = pl.kernel(
        body,
        out_type=jax.ShapeDtypeStruct((bsz * steps * SC_PIECE_ROWS,), F32),
        mesh=plsc.VectorSubcoreMesh(core_axis_name="c", subcore_axis_name="s"),
        scratch_types=[pltpu.VMEM((steps * ppb,), jnp.int32),
                       pltpu.VMEM((2, ppb, SC_PIECE_ROWS, V7X_LANES), F32),
                       pltpu.VMEM((SC_PIECE_ROWS,), F32),
                       pltpu.SemaphoreType.DMA((2,))],
        compiler_params=pltpu.CompilerParams(needs_layout_passes=False),
        name="sc_block_sums",
    )(idx, table)
    return sums.reshape(bsz, nblk, ATTN_W)


def _mlp_kernel(x_ref, a_ref, y_ref, sga_ref, sgc_ref, wba_ref, wbc_ref, wo_ref, g_ref, wup_ref, wdn_ref,
                o_ref, *, chunk):
    a = jnp.dot(a_ref[...], wba_ref[...], preferred_element_type=F32)
    c = jnp.dot(y_ref[...], wbc_ref[...], preferred_element_type=F32)
    mixed = sga_ref[...].astype(F32) * a + sgc_ref[...].astype(F32) * c
    x1 = x_ref[...] + jnp.dot(mixed.astype(BF16), wo_ref[...], preferred_element_type=F32)
    h2 = _rms_rows(x1, g_ref[...]).astype(BF16)
    acc = x1
    for c0 in range(0, wup_ref.shape[1], chunk):
        hid = jnp.maximum(jnp.dot(h2, wup_ref[:, c0:c0 + chunk], preferred_element_type=F32), 0.0)
        acc = acc + jnp.dot((hid * hid).astype(BF16), wdn_ref[c0:c0 + chunk, :], preferred_element_type=F32)
    o_ref[...] = acc


def _merge_mlp(x2d, attn, y, sga, sgc, wba, wbc, wo, g_mlp, wup, wdn):
    n, d = x2d.shape
    cfg = _tiles(n)
    tm = cfg["tm"]
    row = lambda w: pl.BlockSpec((tm, w), lambda i: (i, 0))
    weights = [wba, wbc, wo, g_mlp, wup, wdn]
    return pl.pallas_call(
        functools.partial(_mlp_kernel, chunk=cfg["mlp_chunk"]),
        grid=(n // tm,),
        in_specs=[row(d), row(attn.shape[1]), row(d), row(d), row(d)] + [_resident(w.shape) for w in weights],
        out_specs=row(d),
        out_shape=jax.ShapeDtypeStruct((n, d), F32),
        compiler_params=pltpu.CompilerParams(dimension_semantics=("arbitrary",),
                                             vmem_limit_bytes=cfg["mlp_vmem"]),
        name="merge_mlp",
    )(x2d, attn, y, sga, sgc, *weights)


def _heads_last(xT, lead):
    n = xT.shape[1]
    return jnp.transpose(xT.reshape(N_HEADS, HEAD_DIM, n), (2, 0, 1)).reshape(*lead, N_HEADS, HEAD_DIM)


def kernel(x_prompt, x_sample, cache_k, cache_v, state_conv, page_table, norm_mix, w_in, q_norm, k_norm, conv_w, w_br_attn, w_br_conv, w_out, norm_mlp, w_up, w_down):
    bp, t, d = x_prompt.shape
    db, s, _ = x_sample.shape
    depth = w_in.shape[0]
    n_pool, page = cache_k.shape[1], cache_k.shape[2]
    assert bp == 1 and page == V7X_LANES and t % MOBA_BLOCK == 0
    past = page_table.shape[1] * page
    inv = (ROPE_THETA ** (-jnp.arange(ROT_HALF, dtype=F32) * 2.0 / ROT_DIM)).reshape(ROT_HALF, 1)

    ckT_l = [jnp.transpose(cache_k[l], (0, 2, 3, 1)).reshape(n_pool, ATTN_W, page) for l in range(depth)]
    cvT_l = [jnp.transpose(cache_v[l], (0, 2, 3, 1)).reshape(n_pool, ATTN_W, page) for l in range(depth)]
    ksum = [_sc_block_sums(page_table, ckT_l[l]) for l in range(depth)]

    xp = x_prompt.reshape(t, d)
    xs = x_sample.reshape(db * s, d)
    kp_l, vp_l, cp_l, ks_l, vs_l, cs_l = [], [], [], [], [], []
    for l in range(depth):
        w_qkvT = w_in[l][:, :3 * ATTN_W].T.astype(BF16)
        w_rest = w_in[l][:, 3 * ATTN_W:].astype(BF16)
        g_mix = norm_mix[l].reshape(1, d)
        qg = q_norm[l].reshape(HEAD_DIM, 1)
        kg = k_norm[l].reshape(HEAD_DIM, 1)
        cw = conv_w[l]
        weights = (w_br_attn[l].astype(BF16), w_br_conv[l].astype(BF16), w_out[l].astype(BF16),
                   norm_mlp[l].reshape(1, d), w_up[l].astype(BF16), w_down[l].astype(BF16))

        qT, kT, vT, y, sga, sgc, u_tail, k3, v3, km = _project(xp, g_mix, w_qkvT, w_rest, qg, kg, inv, cw, pos0=0)
        attn = _prompt_attention(qT, km.reshape(t // MOBA_BLOCK, ATTN_W), k3, v3)
        xp = _merge_mlp(xp, attn, y, sga, sgc, *weights)
        kp_l.append(_heads_last(kT, (bp, t)))
        vp_l.append(_heads_last(vT, (bp, t)))
        cp_l.append(u_tail[V7X_SUBLANES - (CONV_K - 1):].reshape(bp, CONV_K - 1, d))

        st = state_conv[l]
        zero = jnp.zeros((db, s, d), F32)
        s1 = zero.at[:, 0].set(st[:, 1]).reshape(db * s, d)
        s2 = zero.at[:, 0].set(st[:, 0]).at[:, 1].set(st[:, 1]).reshape(db * s, d)
        qT, kT, vT, y, sga, sgc, u = _project(xs, g_mix, w_qkvT, w_rest, qg, kg, inv, cw, pos0=past,
                                               prev=(s1, s2, s))
        qkv = jnp.transpose(jnp.stack([qT, kT, vT]).reshape(3, ATTN_W, db, s), (2, 0, 1, 3))
        oT = _decode_attention(page_table, qkv, ksum[l], ckT_l[l], cvT_l[l])
        attn = jnp.transpose(oT, (0, 2, 1)).reshape(db * s, ATTN_W).astype(BF16)
        xs = _merge_mlp(xs, attn, y, sga, sgc, *weights)
        ks_l.append(_heads_last(kT, (db, s)))
        vs_l.append(_heads_last(vT, (db, s)))
        cs_l.append(u.reshape(db, s, d)[:, s - (CONV_K - 1):])

    return (xp.reshape(bp, t, d), xs.reshape(db, s, d), jnp.stack(kp_l), jnp.stack(vp_l), jnp.stack(cp_l),
            jnp.stack(ks_l), jnp.stack(vs_l), jnp.stack(cs_l))
```

```python
import functools

import jax
import jax.numpy as jnp
from jax import lax
from jax.experimental import pallas as pl
from jax.experimental.pallas import tpu as pltpu
from jax.experimental.pallas import tpu_sc as plsc

N_HEADS = 8
HEAD_DIM = 64
ATTN_W = N_HEADS * HEAD_DIM
MOBA_BLOCK = 256
MOBA_TOPK = 3
ROT_DIM = HEAD_DIM // 4
ROT_HALF = ROT_DIM // 2
ROPE_THETA = 500000.0
EPS = 1e-6
NEG = -1e30
CONV_K = 3
SM_SCALE = HEAD_DIM ** -0.5
LOG2E = 1.4426950408889634
BIG = 1e30
V_ROWS = HEAD_DIM + 16

V7X_VMEM_BYTES = 64 * 1024 * 1024
V7X_SUBLANES = 8
V7X_LANES = 128
MIB = 1024 * 1024
V7X_SC_CORES = 2
V7X_SC_SUBCORES = 16
V7X_SC_LANES = 16
SC_PIECE_ROWS = 128

F32 = jnp.float32
BF16 = jnp.bfloat16
HIGHEST = lax.Precision.HIGHEST


def _tiles(n_rows):
    tm = 512 if n_rows % 512 == 0 else n_rows
    return dict(tm=tm, proj_vmem=56 * MIB, attn_vmem=52 * MIB, mlp_vmem=56 * MIB, dec_vmem=56 * MIB,
                attn_q_split=1, attn_look=6, attn_unroll=(8, 4, 1), proj_cols=256, mlp_chunk=1024)


def _resident(shape):
    nd = len(shape)
    return pl.BlockSpec(shape, lambda *_: (0,) * nd, pipeline_mode=pl.Buffered(1))


def _rms_rows(x, g):
    ms = jnp.mean(x * x, axis=-1, keepdims=True)
    return x * lax.rsqrt(ms + EPS) * g


def _sigmoid(x):
    return 1.0 / (1.0 + jnp.exp(-x))


def _proj_kernel(*refs, tm, pos0, pos_mask, prompt, cols):
    if prompt:
        (x_ref, g_ref, wt_ref, wr_ref, qg_ref, kg_ref, inv_ref, cw_ref,
         qT_ref, kT_ref, vT_ref, y_ref, sga_ref, sgc_ref, u_ref,
         k3_ref, v3_ref, km_ref, carry_ref) = refs
    else:
        (x_ref, g_ref, wt_ref, wr_ref, qg_ref, kg_ref, inv_ref, cw_ref, s1_ref, s2_ref,
         qT_ref, kT_ref, vT_ref, y_ref, sga_ref, sgc_ref, u_ref) = refs
    i = pl.program_id(0)
    width = x_ref.shape[1]

    h = _rms_rows(x_ref[...], g_ref[...]).astype(BF16)

    zT = lax.dot_general(wt_ref[...], h, (((1,), (1,)), ((), ())), preferred_element_type=F32)

    col = lax.broadcasted_iota(jnp.int32, (1, tm), 1) + i * tm
    if pos_mask is not None:
        col = col & pos_mask
    ang = inv_ref[...] * (col + pos0).astype(F32)
    cos = jnp.cos(ang)
    sin = jnp.sin(ang)

    def norm_rope(zt, gain):
        outs = []
        for hh in range(N_HEADS):
            blk = zt[hh * HEAD_DIM:(hh + 1) * HEAD_DIM]
            ms = jnp.mean(blk * blk, axis=0, keepdims=True)
            blk = blk * lax.rsqrt(ms + EPS) * gain
            x1 = blk[0:ROT_HALF]
            x2 = blk[ROT_HALF:ROT_DIM]
            outs += [x1 * cos - x2 * sin, x2 * cos + x1 * sin, blk[ROT_DIM:HEAD_DIM]]
        return jnp.concatenate(outs, axis=0)

    qT = norm_rope(zT[0:ATTN_W], qg_ref[...])
    kT = norm_rope(zT[ATTN_W:2 * ATTN_W], kg_ref[...])
    vT = zT[2 * ATTN_W:3 * ATTN_W]
    qT_ref[...] = qT
    kT_ref[...] = kT
    vT_ref[...] = vT

    if prompt:
        k_rows = kT.T
        pad_row = lax.broadcasted_iota(jnp.int32, (V_ROWS - HEAD_DIM, MOBA_BLOCK), 0)
        ones_pad = jnp.where(pad_row == 0, 1.0, 0.0).astype(BF16)
        for g in range(tm // MOBA_BLOCK):
            kb = k_rows[g * MOBA_BLOCK:(g + 1) * MOBA_BLOCK]
            km_ref[0, g:g + 1, :] = jnp.mean(kb, axis=0, keepdims=True)
            k3_ref[g] = kb.astype(BF16)
            vb = vT[:, g * MOBA_BLOCK:(g + 1) * MOBA_BLOCK].astype(BF16)
            pieces = []
            for hh in range(N_HEADS):
                pieces += [vb[hh * HEAD_DIM:(hh + 1) * HEAD_DIM], ones_pad]
            v3_ref[g] = jnp.concatenate(pieces, axis=0)

    if prompt:
        @pl.when(i == 0)
        def _():
            carry_ref[...] = jnp.zeros_like(carry_ref)
    else:
        rm = lax.broadcasted_iota(jnp.int32, (tm, 1), 0) & pos_mask
    for c0 in range(0, width, cols):
        cs = slice(c0, c0 + cols)

        def seg(k, c0=c0):
            return jnp.dot(h, wr_ref[:, k * width + c0:k * width + c0 + cols], preferred_element_type=F32)

        u = seg(2) * seg(0)
        carry = carry_ref[:, cs] if prompt else jnp.zeros((V7X_SUBLANES, cols), F32)
        up = jnp.concatenate([carry, u], axis=0)
        u1 = up[V7X_SUBLANES - 1:V7X_SUBLANES - 1 + tm]
        u2 = up[V7X_SUBLANES - 2:V7X_SUBLANES - 2 + tm]
        if not prompt:
            u1 = jnp.where(rm >= 1, u1, s1_ref[:, cs])
            u2 = jnp.where(rm >= 2, u2, s2_ref[:, cs])
        conv = cw_ref[0:1, cs] * u2 + cw_ref[1:2, cs] * u1 + cw_ref[2:3, cs] * u
        y_ref[:, cs] = (seg(1) * conv).astype(BF16)
        sga_ref[:, cs] = _sigmoid(seg(3)).astype(BF16)
        sgc_ref[:, cs] = _sigmoid(seg(4)).astype(BF16)
        if prompt:
            tail = u[tm - V7X_SUBLANES:tm]
            carry_ref[:, cs] = tail
            u_ref[:, cs] = tail
        else:
            u_ref[:, cs] = u


def _project(x2d, g_mix, w_qkvT, w_rest, qg, kg, inv, cw, *, pos0, prev=None):
    n, d = x2d.shape
    cfg = _tiles(n)
    tm = cfg["tm"]
    prompt = prev is None
    nt = n // tm
    row = lambda w: pl.BlockSpec((tm, w), lambda i: (i, 0))
    colT = pl.BlockSpec((ATTN_W, tm), lambda i: (0, i))
    in_specs = [row(d), _resident(g_mix.shape), _resident(w_qkvT.shape), _resident(w_rest.shape),
                _resident(qg.shape), _resident(kg.shape), _resident(inv.shape), _resident(cw.shape)]
    args = [x2d, g_mix, w_qkvT, w_rest, qg, kg, inv, cw]
    out_shape = [jax.ShapeDtypeStruct((ATTN_W, n), F32)] * 3 + [jax.ShapeDtypeStruct((n, d), BF16)] * 3
    out_specs = [colT, colT, colT, row(d), row(d), row(d)]
    scratch = []
    if prompt:
        assert tm % MOBA_BLOCK == 0
        gpt = tm // MOBA_BLOCK
        out_shape += [jax.ShapeDtypeStruct((V7X_SUBLANES, d), F32),
                      jax.ShapeDtypeStruct((n // MOBA_BLOCK, MOBA_BLOCK, ATTN_W), BF16),
                      jax.ShapeDtypeStruct((n // MOBA_BLOCK, N_HEADS * V_ROWS, MOBA_BLOCK), BF16),
                      jax.ShapeDtypeStruct((nt, gpt, ATTN_W), F32)]
        out_specs += [pl.BlockSpec((V7X_SUBLANES, d), lambda i: (0, 0)),
                      pl.BlockSpec((gpt, MOBA_BLOCK, ATTN_W), lambda i: (i, 0, 0)),
                      pl.BlockSpec((gpt, N_HEADS * V_ROWS, MOBA_BLOCK), lambda i: (i, 0, 0)),
                      pl.BlockSpec((1, gpt, ATTN_W), lambda i: (i, 0, 0))]
        scratch = [pltpu.VMEM((V7X_SUBLANES, d), F32)]
        pos_mask = None
    else:
        s1, s2, seq = prev
        assert seq & (seq - 1) == 0
        pos_mask = seq - 1
        in_specs += [row(d), row(d)]
        args += [s1, s2]
        out_shape += [jax.ShapeDtypeStruct((n, d), F32)]
        out_specs += [row(d)]
    return pl.pallas_call(
        functools.partial(_proj_kernel, tm=tm, pos0=pos0, pos_mask=pos_mask, prompt=prompt,
                          cols=cfg["proj_cols"]),
        grid=(nt,), in_specs=in_specs, out_specs=out_specs, out_shape=out_shape, scratch_shapes=scratch,
        compiler_params=pltpu.CompilerParams(dimension_semantics=("arbitrary",),
                                             vmem_limit_bytes=cfg["proj_vmem"]),
        name="proj_prompt" if prompt else "proj_sample",
    )(*args)


def _attn_kernel(q_ref, km_ref, k_hbm, v_hbm, o_ref,
                 kbuf, vbuf, qm_ref, sel_ref, acc_ref, m_ref, sem, *, nb, q_split, look, unroll):
    i = pl.program_id(0)
    blk_q = MOBA_BLOCK

    @pl.when(i == 0)
    def _():
        ck = pltpu.make_async_copy(k_hbm, kbuf, sem.at[0])
        cv = pltpu.make_async_copy(v_hbm, vbuf, sem.at[1])
        ck.start()
        cv.start()
        ck.wait()
        cv.wait()

    pair_w = 2 * HEAD_DIM
    blk = lax.broadcasted_iota(jnp.int32, (nb, blk_q), 0)
    rid = lax.broadcasted_iota(jnp.int32, (pair_w, 1), 0)
    for h in range(N_HEADS):
        pr = h // 2
        qp = q_ref[pr * pair_w:(pr + 1) * pair_w, :]
        keep = (rid >= HEAD_DIM) if h % 2 else (rid < HEAD_DIM)
        qm = jnp.where(keep, qp, 0.0)
        gate = jnp.dot(km_ref[:, pr * pair_w:(pr + 1) * pair_w], qm,
                       precision=HIGHEST, preferred_element_type=F32)
        g = jnp.where(blk < i, gate, -jnp.inf)
        sel = jnp.zeros((nb, blk_q), F32)
        for r in range(MOBA_TOPK):
            mx = jnp.max(g, axis=0, keepdims=True)
            idx = jnp.min(jnp.where(g == mx, blk, nb), axis=0, keepdims=True)
            hit = blk == idx
            sel = jnp.where(jnp.logical_and(hit, i > r), 1.0, sel)
            g = jnp.where(hit, -jnp.inf, g)
        sel_ref[h] = sel
        qm_ref[h] = (qm * (SM_SCALE * LOG2E)).astype(BF16)

    qw = blk_q // q_split

    def scores(j, h, c):
        pr = h // 2
        return jnp.dot(kbuf[j, :, pr * pair_w:(pr + 1) * pair_w], qm_ref[h, :, c * qw:(c + 1) * qw],
                       preferred_element_type=F32)

    def pv(j, h, p):
        return jnp.dot(vbuf[j, h * V_ROWS:(h + 1) * V_ROWS, :], p.astype(BF16), preferred_element_type=F32)

    kq = lax.broadcasted_iota(jnp.int32, (blk_q, qw), 0)
    qq = lax.broadcasted_iota(jnp.int32, (blk_q, qw), 1)

    def run_tiles(js, own):
        tiles = [(j, h, c) for j in js for h in range(N_HEADS) for c in range(q_split)]
        pending = {}
        for k in range(len(tiles) + look):
            if k < len(tiles):
                pending[k] = scores(*tiles[k])
            if k < look:
                continue
            j, h, c = tiles[k - look]
            s = pending.pop(k - look)
            cols = slice(c * qw, (c + 1) * qw)
            rows = slice(h * V_ROWS, (h + 1) * V_ROWS)
            if own:
                s = jnp.where(kq <= qq + c * qw, s, NEG)
                m = jnp.max(s, axis=0, keepdims=True)
                m_ref[h:h + 1, cols] = m
                acc_ref[rows, cols] = pv(j, h, jnp.exp2(s - m))
            else:
                picked = sel_ref[h, pl.ds(j, 1), :][:, cols] > 0.0
                m_old = m_ref[h:h + 1, cols]
                m_new = jnp.where(picked, jnp.maximum(m_old, jnp.max(s, axis=0, keepdims=True)), m_old)
                m_ref[h:h + 1, cols] = m_new
                part = pv(j, h, jnp.exp2(s - jnp.where(picked, m_new, BIG)))
                acc_ref[rows, cols] = jnp.exp2(m_old - m_new) * acc_ref[rows, cols] + part

    run_tiles([i], own=True)

    done = 0
    for u in unroll:
        trips = (i - done) // u

        def past_body(jj, c, u=u, done=done):
            run_tiles([done + u * jj + b for b in range(u)], own=False)
            return c

        lax.fori_loop(0, trips, past_body, 0)
        done = done + trips * u

    outs = []
    for h in range(N_HEADS):
        a = acc_ref[h * V_ROWS:(h + 1) * V_ROWS, :]
        outs.append(a[0:HEAD_DIM] / a[HEAD_DIM:HEAD_DIM + 1])
    o_ref[...] = jnp.concatenate(outs, axis=0).T.astype(BF16)


def _prompt_attention(qT, kmeans, k3, v3):
    nb = k3.shape[0]
    t = nb * MOBA_BLOCK
    cfg = _tiles(t)
    return pl.pallas_call(
        functools.partial(_attn_kernel, nb=nb, q_split=cfg["attn_q_split"], look=cfg["attn_look"],
                          unroll=cfg["attn_unroll"]),
        grid=(nb,),
        in_specs=[pl.BlockSpec((ATTN_W, MOBA_BLOCK), lambda i: (0, i)),
                  _resident(kmeans.shape),
                  pl.BlockSpec(memory_space=pl.ANY),
                  pl.BlockSpec(memory_space=pl.ANY)],
        out_specs=pl.BlockSpec((MOBA_BLOCK, ATTN_W), lambda i: (i, 0)),
        out_shape=jax.ShapeDtypeStruct((t, ATTN_W), BF16),
        scratch_shapes=[pltpu.VMEM(k3.shape, BF16), pltpu.VMEM(v3.shape, BF16),
                        pltpu.VMEM((N_HEADS, 2 * HEAD_DIM, MOBA_BLOCK), BF16),
                        pltpu.VMEM((N_HEADS, nb, MOBA_BLOCK), F32),
                        pltpu.VMEM((N_HEADS * V_ROWS, MOBA_BLOCK), F32),
                        pltpu.VMEM((N_HEADS, MOBA_BLOCK), F32),
                        pltpu.SemaphoreType.DMA((2,))],
        compiler_params=pltpu.CompilerParams(dimension_semantics=("arbitrary",),
                                             vmem_limit_bytes=cfg["attn_vmem"]),
        name="moba_prompt",
    )(qT, kmeans, k3, v3)


def _dec_kernel(pt_ref, ksum_ref, qT_ref, kT_ref, vT_ref, ck_hbm, cv_hbm, o_ref,
                idxv_ref, idxs_ref, kg, vg, vlast, sem_g, sem_misc, *, n_pages, seq):
    b = pl.program_id(0)
    n_tok = qT_ref.shape[1]

    def seq_cols(ref, sq, width):
        x = pltpu.roll(ref[...], lax.rem(n_tok - seq * sq, n_tok), axis=1)
        if n_tok < width:
            x = jnp.concatenate([x, jnp.zeros((x.shape[0], width - n_tok), x.dtype)], axis=1)
        return x[:, 0:width]
    n_seq = pl.num_programs(0) - 1
    ppb = MOBA_BLOCK // V7X_LANES
    nblk = n_pages // ppb
    per_q = MOBA_TOPK * ppb
    n_tiles = seq * per_q

    def head_rows(fn, width):
        sub = lax.broadcasted_iota(jnp.int32, (N_HEADS, width), 0)
        out = jnp.broadcast_to(fn(0), (N_HEADS, width))
        for h in range(1, N_HEADS):
            out = jnp.where(sub == h, fn(h), out)
        return out

    def hs(h):
        return slice(h * HEAD_DIM, (h + 1) * HEAD_DIM)

    def tile_copies(st, t, h, page):
        return (pltpu.make_async_copy(ck_hbm.at[page, hs(h)], kg.at[st, t, hs(h)], sem_g.at[st, 0]),
                pltpu.make_async_copy(cv_hbm.at[page, hs(h)], vg.at[st, t, hs(h)], sem_g.at[st, 1]))

    def last_copy(st, page):
        return pltpu.make_async_copy(cv_hbm.at[page], vlast.at[st], sem_misc.at[1 + st])

    @pl.when(b < n_seq)
    def _():
        st = b % 2
        qb = seq_cols(qT_ref, b, V7X_LANES)
        lane_head = lax.broadcasted_iota(jnp.int32, (HEAD_DIM, V7X_LANES), 1) // seq
        qbd = jnp.concatenate(
            [jnp.where(lane_head == h, pltpu.roll(qb[hs(h)], seq * h, axis=1), 0.0) for h in range(N_HEADS)],
            axis=0)
        g = jnp.dot(ksum_ref[0], qbd, precision=HIGHEST, preferred_element_type=F32)
        g = g * (1.0 / MOBA_BLOCK)
        blk = lax.broadcasted_iota(jnp.int32, g.shape, 0)
        sub8 = lax.broadcasted_iota(jnp.int32, (V7X_SUBLANES, V7X_LANES), 0)
        idxv = jnp.zeros((V7X_SUBLANES, V7X_LANES), jnp.int32)
        for r in range(MOBA_TOPK):
            mx = jnp.max(g, axis=0, keepdims=True)
            idx = jnp.min(jnp.where(g == mx, blk, nblk), axis=0, keepdims=True)
            idxv = jnp.where(sub8 == r, idx, idxv)
            g = jnp.where(blk == idx, -jnp.inf, g)
        idxv_ref[...] = idxv
        cp = pltpu.make_async_copy(idxv_ref, idxs_ref, sem_misc.at[0])
        cp.start()
        cp.wait()

        def issue(t, c):
            sr = lax.div(t, ppb)
            pg = t - sr * ppb
            s = lax.div(sr, MOBA_TOPK)
            r = sr - s * MOBA_TOPK
            for h in range(N_HEADS):
                ck, cv = tile_copies(st, t, h, pt_ref[b, idxs_ref[r, h * seq + s] * ppb + pg])
                ck.start()
                cv.start()
            return c

        lax.fori_loop(0, n_tiles, issue, 0)
        last_copy(st, pt_ref[b, n_pages - 1]).start()

    @pl.when(b >= 1)
    def _():
        st = (b - 1) % 2
        for _ in range(n_tiles * N_HEADS):
            ck, cv = tile_copies(st, 0, 0, 0)
            ck.wait()
            cv.wait()
        last_copy(st, 0).wait()

        q_new, k_new, v_new = (seq_cols(ref, b - 1, seq) for ref in (qT_ref, kT_ref, vT_ref))
        lane_s = lax.broadcasted_iota(jnp.int32, (N_HEADS, seq), 1)
        cols = []
        for s in range(seq):
            qcol = q_new[:, s:s + 1] * SM_SCALE

            def tile_scores(t, s=s, qcol=qcol):
                return head_rows(
                    lambda h: jnp.sum(kg[st, s * per_q + t, hs(h), :] * qcol[hs(h)], axis=0, keepdims=True),
                    V7X_LANES)

            scs = [tile_scores(t) for t in range(per_q)]
            loc = head_rows(lambda h: jnp.sum(k_new[hs(h)] * qcol[hs(h)], axis=0, keepdims=True), seq)
            loc = jnp.where(lane_s <= s, loc, NEG)
            mt = scs[0]
            for sc in scs[1:]:
                mt = jnp.maximum(mt, sc)
            m = jnp.maximum(jnp.max(mt, axis=1, keepdims=True), jnp.max(loc, axis=1, keepdims=True))
            ps = [jnp.exp(sc - m) for sc in scs]
            ploc = jnp.exp(loc - m)
            plast = jnp.exp(NEG - m)
            psum = ps[0]
            for p_ in ps[1:]:
                psum = psum + p_
            l = jnp.sum(psum, axis=1, keepdims=True) + jnp.sum(ploc, axis=1, keepdims=True) + V7X_LANES * plast
            rl = 1.0 / l
            outs = []
            for h in range(N_HEADS):
                acc = vlast[st, hs(h), :] * plast[h:h + 1, :]
                for t in range(per_q):
                    acc = acc + vg[st, s * per_q + t, hs(h), :] * ps[t][h:h + 1, :]
                o = jnp.sum(acc, axis=1, keepdims=True)
                o = o + jnp.sum(v_new[hs(h)] * ploc[h:h + 1, :], axis=1, keepdims=True)
                outs.append(o * rl[h:h + 1, :])
            cols.append(jnp.concatenate(outs, axis=0))
        o_ref[0] = jnp.concatenate(cols, axis=1)


def _decode_attention(page_table, qT, kT, vT, ksum, ckT, cvT, *, seq):
    bsz = ksum.shape[0]
    n_pages = page_table.shape[1]
    ppb = MOBA_BLOCK // V7X_LANES
    nblk = n_pages // ppb
    assert ckT.shape[1:] == (ATTN_W, V7X_LANES) and n_pages % ppb == 0 and ksum.shape == (bsz, nblk, ATTN_W)
    assert MOBA_TOPK <= V7X_SUBLANES and N_HEADS * seq <= V7X_LANES and qT.shape == (ATTN_W, bsz * seq)
    n_tiles = seq * MOBA_TOPK * ppb
    cfg = _tiles(bsz * seq)
    tok = _resident(qT.shape)
    grid_spec = pltpu.PrefetchScalarGridSpec(
        num_scalar_prefetch=1, grid=(bsz + 1,),
        in_specs=[pl.BlockSpec((1, nblk, ATTN_W), lambda b, pt: (jnp.minimum(b, bsz - 1), 0, 0)),
                  tok, tok, tok,
                  pl.BlockSpec(memory_space=pl.ANY), pl.BlockSpec(memory_space=pl.ANY)],
        out_specs=pl.BlockSpec((1, ATTN_W, seq), lambda b, pt: (jnp.maximum(b - 1, 0), 0, 0)),
        scratch_shapes=[pltpu.VMEM((N_HEADS, V7X_LANES), jnp.int32),
                        pltpu.SMEM((N_HEADS, V7X_LANES), jnp.int32),
                        pltpu.VMEM((2, n_tiles, ATTN_W, V7X_LANES), F32),
                        pltpu.VMEM((2, n_tiles, ATTN_W, V7X_LANES), F32),
                        pltpu.VMEM((2, ATTN_W, V7X_LANES), F32),
                        pltpu.SemaphoreType.DMA((2, 2)),
                        pltpu.SemaphoreType.DMA((3,))])
    return pl.pallas_call(
        functools.partial(_dec_kernel, n_pages=n_pages, seq=seq),
        grid_spec=grid_spec,
        out_shape=jax.ShapeDtypeStruct((bsz, ATTN_W, seq), F32),
        compiler_params=pltpu.CompilerParams(dimension_semantics=("arbitrary",),
                                             vmem_limit_bytes=cfg["dec_vmem"]),
        name="moba_decode",
    )(page_table, ksum, qT, kT, vT, ckT, cvT)


def _sc_block_sums(page_table, ckT):
    bsz, n_pages = page_table.shape
    n_pool = ckT.shape[0]
    ppb = MOBA_BLOCK // V7X_LANES
    nblk = n_pages // ppb
    n_workers = V7X_SC_CORES * V7X_SC_SUBCORES
    assert bsz % n_workers == 0 and ATTN_W % SC_PIECE_ROWS == 0
    seq_per_worker = bsz // n_workers
    pieces = ATTN_W // SC_PIECE_ROWS
    steps = nblk * pieces
    table = ckT.reshape(n_pool * pieces, SC_PIECE_ROWS, V7X_LANES)
    idx = (page_table.reshape(bsz, nblk, 1, ppb) * pieces
           + jnp.arange(pieces, dtype=jnp.int32).reshape(1, 1, pieces, 1)).reshape(-1)
    lanes_per_row = V7X_LANES // V7X_SC_LANES

    def body(idx_hbm, tab_hbm, out_hbm, idx_v, buf, orow, sem):
        wid = lax.axis_index("c") * V7X_SC_SUBCORES + lax.axis_index("s")
        lane = lax.iota(jnp.int32, V7X_SC_LANES)

        def fetch(n, slot):
            return pltpu.make_async_copy(tab_hbm.at[idx_v.at[pl.ds(n * ppb, ppb)]], buf.at[slot], sem.at[slot])

        for q in range(seq_per_worker):
            sq = wid * seq_per_worker + q
            pltpu.sync_copy(idx_hbm.at[pl.ds(sq * (steps * ppb), steps * ppb)], idx_v)
            fetch(0, 0).start()

            @pl.loop(0, steps, step=2)
            def _(n0):
                for slot in range(2):
                    n = n0 + slot

                    @pl.when(n + 1 < steps)
                    def _():
                        fetch(n + 1, 1 - slot).start()

                    fetch(n, slot).wait()

                    @pl.loop(0, SC_PIECE_ROWS, step=V7X_SC_LANES)
                    def _(r0):
                        vec = jnp.zeros((V7X_SC_LANES,), F32)
                        for i in range(V7X_SC_LANES):
                            acc = None
                            for pg in range(ppb):
                                for k in range(lanes_per_row):
                                    v = buf[slot, pg, r0 + i, pl.ds(k * V7X_SC_LANES, V7X_SC_LANES)]
                                    acc = v if acc is None else acc + v
                            vec = jnp.where(lane == i, jnp.sum(acc), vec)
                        orow[pl.ds(r0, V7X_SC_LANES)] = vec

                    pltpu.sync_copy(orow, out_hbm.at[pl.ds((sq * steps + n) * SC_PIECE_ROWS, SC_PIECE_ROWS)])

    sums = pl.kernel(
        body,
        out_type=jax.ShapeDtypeStruct((bsz * steps * SC_PIECE_ROWS,), F32),
        mesh=plsc.VectorSubcoreMesh(core_axis_name="c", subcore_axis_name="s"),
        scratch_types=[pltpu.VMEM((steps * ppb,), jnp.int32),
                       pltpu.VMEM((2, ppb, SC_PIECE_ROWS, V7X_LANES), F32),
                       pltpu.VMEM((SC_PIECE_ROWS,), F32),
                       pltpu.SemaphoreType.DMA((2,))],
        compiler_params=pltpu.CompilerParams(needs_layout_passes=False),
        name="sc_block_sums",
    )(idx, table)
    return sums.reshape(bsz, nblk, ATTN_W)


def _mlp_kernel(x_ref, a_ref, y_ref, sga_ref, sgc_ref, wba_ref, wbc_ref, wo_ref, g_ref, wup_ref, wdn_ref,
                o_ref, *, chunk):
    a = jnp.dot(a_ref[...], wba_ref[...], preferred_element_type=F32)
    c = jnp.dot(y_ref[...], wbc_ref[...], preferred_element_type=F32)
    mixed = sga_ref[...].astype(F32) * a + sgc_ref[...].astype(F32) * c
    x1 = x_ref[...] + jnp.dot(mixed.astype(BF16), wo_ref[...], preferred_element_type=F32)
    h2 = _rms_rows(x1, g_ref[...]).astype(BF16)
    acc = x1
    for c0 in range(0, wup_ref.shape[1], chunk):
        hid = jnp.maximum(jnp.dot(h2, wup_ref[:, c0:c0 + chunk], preferred_element_type=F32), 0.0)
        acc = acc + jnp.dot((hid * hid).astype(BF16), wdn_ref[c0:c0 + chunk, :], preferred_element_type=F32)
    o_ref[...] = acc


def _merge_mlp(x2d, attn, y, sga, sgc, wba, wbc, wo, g_mlp, wup, wdn):
    n, d = x2d.shape
    cfg = _tiles(n)
    tm = cfg["tm"]
    row = lambda w: pl.BlockSpec((tm, w), lambda i: (i, 0))
    weights = [wba, wbc, wo, g_mlp, wup, wdn]
    return pl.pallas_call(
        functools.partial(_mlp_kernel, chunk=cfg["mlp_chunk"]),
        grid=(n // tm,),
        in_specs=[row(d), row(attn.shape[1]), row(d), row(d), row(d)] + [_resident(w.shape) for w in weights],
        out_specs=row(d),
        out_shape=jax.ShapeDtypeStruct((n, d), F32),
        compiler_params=pltpu.CompilerParams(dimension_semantics=("arbitrary",),
                                             vmem_limit_bytes=cfg["mlp_vmem"]),
        name="merge_mlp",
    )(x2d, attn, y, sga, sgc, *weights)


def _heads_last(xT, lead):
    n = xT.shape[1]
    return jnp.transpose(xT.reshape(N_HEADS, HEAD_DIM, n), (2, 0, 1)).reshape(*lead, N_HEADS, HEAD_DIM)


def kernel(x_prompt, x_sample, cache_k, cache_v, state_conv, page_table, norm_mix, w_in, q_norm, k_norm, conv_w, w_br_attn, w_br_conv, w_out, norm_mlp, w_up, w_down):
    bp, t, d = x_prompt.shape
    db, s, _ = x_sample.shape
    depth = w_in.shape[0]
    n_pool, page = cache_k.shape[1], cache_k.shape[2]
    assert bp == 1 and page == V7X_LANES and t % MOBA_BLOCK == 0
    past = page_table.shape[1] * page
    inv = (ROPE_THETA ** (-jnp.arange(ROT_HALF, dtype=F32) * 2.0 / ROT_DIM)).reshape(ROT_HALF, 1)

    ckT_l = [jnp.transpose(cache_k[l], (0, 2, 3, 1)).reshape(n_pool, ATTN_W, page) for l in range(depth)]
    cvT_l = [jnp.transpose(cache_v[l], (0, 2, 3, 1)).reshape(n_pool, ATTN_W, page) for l in range(depth)]
    ksum = [_sc_block_sums(page_table, ckT_l[l]) for l in range(depth)]

    xp = x_prompt.reshape(t, d)
    xs = x_sample.reshape(db * s, d)
    kp_l, vp_l, cp_l, ks_l, vs_l, cs_l = [], [], [], [], [], []
    for l in range(depth):
        w_qkvT = w_in[l][:, :3 * ATTN_W].T.astype(BF16)
        w_rest = w_in[l][:, 3 * ATTN_W:].astype(BF16)
        g_mix = norm_mix[l].reshape(1, d)
        qg = q_norm[l].reshape(HEAD_DIM, 1)
        kg = k_norm[l].reshape(HEAD_DIM, 1)
        cw = conv_w[l]
        weights = (w_br_attn[l].astype(BF16), w_br_conv[l].astype(BF16), w_out[l].astype(BF16),
                   norm_mlp[l].reshape(1, d), w_up[l].astype(BF16), w_down[l].astype(BF16))

        qT, kT, vT, y, sga, sgc, u_tail, k3, v3, km = _project(xp, g_mix, w_qkvT, w_rest, qg, kg, inv, cw, pos0=0)
        attn = _prompt_attention(qT, km.reshape(t // MOBA_BLOCK, ATTN_W), k3, v3)
        xp = _merge_mlp(xp, attn, y, sga, sgc, *weights)
        kp_l.append(_heads_last(kT, (bp, t)))
        vp_l.append(_heads_last(vT, (bp, t)))
        cp_l.append(u_tail[V7X_SUBLANES - (CONV_K - 1):].reshape(bp, CONV_K - 1, d))

        st = state_conv[l]
        zero = jnp.zeros((db, s, d), F32)
        s1 = zero.at[:, 0].set(st[:, 1]).reshape(db * s, d)
        s2 = zero.at[:, 0].set(st[:, 0]).at[:, 1].set(st[:, 1]).reshape(db * s, d)
        qT, kT, vT, y, sga, sgc, u = _project(xs, g_mix, w_qkvT, w_rest, qg, kg, inv, cw, pos0=past,
                                               prev=(s1, s2, s))
        oT = _decode_attention(page_table, qT, kT, vT, ksum[l], ckT_l[l], cvT_l[l], seq=s)
        attn = jnp.transpose(oT, (0, 2, 1)).reshape(db * s, ATTN_W).astype(BF16)
        xs = _merge_mlp(xs, attn, y, sga, sgc, *weights)
        ks_l.append(_heads_last(kT, (db, s)))
        vs_l.append(_heads_last(vT, (db, s)))
        cs_l.append(u.reshape(db, s, d)[:, s - (CONV_K - 1):])

    return (xp.reshape(bp, t, d), xs.reshape(db, s, d), jnp.stack(kp_l), jnp.stack(vp_l), jnp.stack(cp_l),
            jnp.stack(ks_l), jnp.stack(vs_l), jnp.stack(cs_l))
```

```python
import functools

import jax
import jax.numpy as jnp
from jax import lax
from jax.experimental import pallas as pl
from jax.experimental.pallas import tpu as pltpu
from jax.experimental.pallas import tpu_sc as plsc

N_HEADS = 8
HEAD_DIM = 64
ATTN_W = N_HEADS * HEAD_DIM
MOBA_BLOCK = 256
MOBA_TOPK = 3
ROT_DIM = HEAD_DIM // 4
ROT_HALF = ROT_DIM // 2
ROPE_THETA = 500000.0
EPS = 1e-6
NEG = -1e30
CONV_K = 3
SM_SCALE = HEAD_DIM ** -0.5
LOG2E = 1.4426950408889634
BIG = 1e30
V_ROWS = HEAD_DIM + 16

V7X_VMEM_BYTES = 64 * 1024 * 1024
V7X_SUBLANES = 8
V7X_LANES = 128
MIB = 1024 * 1024
V7X_SC_CORES = 2
V7X_SC_SUBCORES = 16
V7X_SC_LANES = 16
SC_PIECE_ROWS = 128

F32 = jnp.float32
BF16 = jnp.bfloat16
HIGHEST = lax.Precision.HIGHEST


def _tiles(n_rows):
    tm = 512 if n_rows % 512 == 0 else n_rows
    return dict(tm=tm, proj_vmem=56 * MIB, attn_vmem=52 * MIB, mlp_vmem=56 * MIB, dec_vmem=56 * MIB,
                attn_q_split=1, attn_look=6, attn_unroll=(16, 8, 4, 1), proj_cols=256, mlp_chunk=1024)


def _resident(shape):
    nd = len(shape)
    return pl.BlockSpec(shape, lambda *_: (0,) * nd, pipeline_mode=pl.Buffered(1))


def _rms_rows(x, g):
    ms = jnp.mean(x * x, axis=-1, keepdims=True)
    return x * lax.rsqrt(ms + EPS) * g


def _sigmoid(x):
    return 1.0 / (1.0 + jnp.exp(-x))


def _proj_kernel(*refs, tm, pos0, pos_mask, prompt, cols):
    if prompt:
        (x_ref, g_ref, wt_ref, wr_ref, qg_ref, kg_ref, inv_ref, cw_ref,
         qT_ref, kT_ref, vT_ref, y_ref, sga_ref, sgc_ref, u_ref,
         k3_ref, v3_ref, km_ref, carry_ref) = refs
    else:
        (x_ref, g_ref, wt_ref, wr_ref, qg_ref, kg_ref, inv_ref, cw_ref, s1_ref, s2_ref,
         qT_ref, kT_ref, vT_ref, y_ref, sga_ref, sgc_ref, u_ref) = refs
    i = pl.program_id(0)
    width = x_ref.shape[1]

    h = _rms_rows(x_ref[...], g_ref[...]).astype(BF16)

    zT = lax.dot_general(wt_ref[...], h, (((1,), (1,)), ((), ())), preferred_element_type=F32)

    col = lax.broadcasted_iota(jnp.int32, (1, tm), 1) + i * tm
    if pos_mask is not None:
        col = col & pos_mask
    ang = inv_ref[...] * (col + pos0).astype(F32)
    cos = jnp.cos(ang)
    sin = jnp.sin(ang)

    def norm_rope(zt, gain):
        outs = []
        for hh in range(N_HEADS):
            blk = zt[hh * HEAD_DIM:(hh + 1) * HEAD_DIM]
            ms = jnp.mean(blk * blk, axis=0, keepdims=True)
            blk = blk * lax.rsqrt(ms + EPS) * gain
            x1 = blk[0:ROT_HALF]
            x2 = blk[ROT_HALF:ROT_DIM]
            outs += [x1 * cos - x2 * sin, x2 * cos + x1 * sin, blk[ROT_DIM:HEAD_DIM]]
        return jnp.concatenate(outs, axis=0)

    qT = norm_rope(zT[0:ATTN_W], qg_ref[...])
    kT = norm_rope(zT[ATTN_W:2 * ATTN_W], kg_ref[...])
    vT = zT[2 * ATTN_W:3 * ATTN_W]
    qT_ref[...] = qT
    kT_ref[...] = kT
    vT_ref[...] = vT

    if prompt:
        k_rows = kT.T
        pad_row = lax.broadcasted_iota(jnp.int32, (V_ROWS - HEAD_DIM, MOBA_BLOCK), 0)
        ones_pad = jnp.where(pad_row == 0, 1.0, 0.0).astype(BF16)
        for g in range(tm // MOBA_BLOCK):
            kb = k_rows[g * MOBA_BLOCK:(g + 1) * MOBA_BLOCK]
            km_ref[0, g:g + 1, :] = jnp.mean(kb, axis=0, keepdims=True)
            k3_ref[g] = kb.astype(BF16)
            vb = vT[:, g * MOBA_BLOCK:(g + 1) * MOBA_BLOCK].astype(BF16)
            pieces = []
            for hh in range(N_HEADS):
                pieces += [vb[hh * HEAD_DIM:(hh + 1) * HEAD_DIM], ones_pad]
            v3_ref[g] = jnp.concatenate(pieces, axis=0)

    if prompt:
        @pl.when(i == 0)
        def _():
            carry_ref[...] = jnp.zeros_like(carry_ref)
    else:
        rm = lax.broadcasted_iota(jnp.int32, (tm, 1), 0) & pos_mask
    for c0 in range(0, width, cols):
        cs = slice(c0, c0 + cols)

        def seg(k, c0=c0):
            return jnp.dot(h, wr_ref[:, k * width + c0:k * width + c0 + cols], preferred_element_type=F32)

        u = seg(2) * seg(0)
        carry = carry_ref[:, cs] if prompt else jnp.zeros((V7X_SUBLANES, cols), F32)
        up = jnp.concatenate([carry, u], axis=0)
        u1 = up[V7X_SUBLANES - 1:V7X_SUBLANES - 1 + tm]
        u2 = up[V7X_SUBLANES - 2:V7X_SUBLANES - 2 + tm]
        if not prompt:
            u1 = jnp.where(rm >= 1, u1, s1_ref[:, cs])
            u2 = jnp.where(rm >= 2, u2, s2_ref[:, cs])
        conv = cw_ref[0:1, cs] * u2 + cw_ref[1:2, cs] * u1 + cw_ref[2:3, cs] * u
        y_ref[:, cs] = (seg(1) * conv).astype(BF16)
        sga_ref[:, cs] = _sigmoid(seg(3)).astype(BF16)
        sgc_ref[:, cs] = _sigmoid(seg(4)).astype(BF16)
        if prompt:
            tail = u[tm - V7X_SUBLANES:tm]
            carry_ref[:, cs] = tail
            u_ref[:, cs] = tail
        else:
            u_ref[:, cs] = u


def _project(x2d, g_mix, w_qkvT, w_rest, qg, kg, inv, cw, *, pos0, prev=None):
    n, d = x2d.shape
    cfg = _tiles(n)
    tm = cfg["tm"]
    prompt = prev is None
    nt = n // tm
    row = lambda w: pl.BlockSpec((tm, w), lambda i: (i, 0))
    colT = pl.BlockSpec((ATTN_W, tm), lambda i: (0, i))
    in_specs = [row(d), _resident(g_mix.shape), _resident(w_qkvT.shape), _resident(w_rest.shape),
                _resident(qg.shape), _resident(kg.shape), _resident(inv.shape), _resident(cw.shape)]
    args = [x2d, g_mix, w_qkvT, w_rest, qg, kg, inv, cw]
    out_shape = [jax.ShapeDtypeStruct((ATTN_W, n), F32)] * 3 + [jax.ShapeDtypeStruct((n, d), BF16)] * 3
    out_specs = [colT, colT, colT, row(d), row(d), row(d)]
    scratch = []
    if prompt:
        assert tm % MOBA_BLOCK == 0
        gpt = tm // MOBA_BLOCK
        out_shape += [jax.ShapeDtypeStruct((V7X_SUBLANES, d), F32),
                      jax.ShapeDtypeStruct((n // MOBA_BLOCK, MOBA_BLOCK, ATTN_W), BF16),
                      jax.ShapeDtypeStruct((n // MOBA_BLOCK, N_HEADS * V_ROWS, MOBA_BLOCK), BF16),
                      jax.ShapeDtypeStruct((nt, gpt, ATTN_W), F32)]
        out_specs += [pl.BlockSpec((V7X_SUBLANES, d), lambda i: (0, 0)),
                      pl.BlockSpec((gpt, MOBA_BLOCK, ATTN_W), lambda i: (i, 0, 0)),
                      pl.BlockSpec((gpt, N_HEADS * V_ROWS, MOBA_BLOCK), lambda i: (i, 0, 0)),
                      pl.BlockSpec((1, gpt, ATTN_W), lambda i: (i, 0, 0))]
        scratch = [pltpu.VMEM((V7X_SUBLANES, d), F32)]
        pos_mask = None
    else:
        s1, s2, seq = prev
        assert seq & (seq - 1) == 0
        pos_mask = seq - 1
        in_specs += [row(d), row(d)]
        args += [s1, s2]
        out_shape += [jax.ShapeDtypeStruct((n, d), F32)]
        out_specs += [row(d)]
    return pl.pallas_call(
        functools.partial(_proj_kernel, tm=tm, pos0=pos0, pos_mask=pos_mask, prompt=prompt,
                          cols=cfg["proj_cols"]),
        grid=(nt,), in_specs=in_specs, out_specs=out_specs, out_shape=out_shape, scratch_shapes=scratch,
        compiler_params=pltpu.CompilerParams(dimension_semantics=("arbitrary",),
                                             vmem_limit_bytes=cfg["proj_vmem"]),
        name="proj_prompt" if prompt else "proj_sample",
    )(*args)


def _attn_kernel(q_ref, km_ref, k_hbm, v_hbm, o_ref,
                 kbuf, vbuf, qm_ref, sel_ref, acc_ref, m_ref, sem, *, nb, q_split, look, unroll):
    i = pl.program_id(0)
    blk_q = MOBA_BLOCK

    @pl.when(i == 0)
    def _():
        ck = pltpu.make_async_copy(k_hbm, kbuf, sem.at[0])
        cv = pltpu.make_async_copy(v_hbm, vbuf, sem.at[1])
        ck.start()
        cv.start()
        ck.wait()
        cv.wait()

    pair_w = 2 * HEAD_DIM
    blk = lax.broadcasted_iota(jnp.int32, (nb, blk_q), 0)
    rid = lax.broadcasted_iota(jnp.int32, (pair_w, 1), 0)
    for h in range(N_HEADS):
        pr = h // 2
        qp = q_ref[pr * pair_w:(pr + 1) * pair_w, :]
        keep = (rid >= HEAD_DIM) if h % 2 else (rid < HEAD_DIM)
        qm = jnp.where(keep, qp, 0.0)
        gate = jnp.dot(km_ref[:, pr * pair_w:(pr + 1) * pair_w], qm,
                       precision=HIGHEST, preferred_element_type=F32)
        g = jnp.where(blk < i, gate, -jnp.inf)
        sel = jnp.zeros((nb, blk_q), F32)
        for r in range(MOBA_TOPK):
            mx = jnp.max(g, axis=0, keepdims=True)
            idx = jnp.min(jnp.where(g == mx, blk, nb), axis=0, keepdims=True)
            hit = blk == idx
            sel = jnp.where(jnp.logical_and(hit, i > r), 1.0, sel)
            g = jnp.where(hit, -jnp.inf, g)
        sel_ref[h] = sel
        qm_ref[h] = (qm * (SM_SCALE * LOG2E)).astype(BF16)

    qw = blk_q // q_split

    def scores(j, h, c):
        pr = h // 2
        return jnp.dot(kbuf[j, :, pr * pair_w:(pr + 1) * pair_w], qm_ref[h, :, c * qw:(c + 1) * qw],
                       preferred_element_type=F32)

    def pv(j, h, p):
        return jnp.dot(vbuf[j, h * V_ROWS:(h + 1) * V_ROWS, :], p.astype(BF16), preferred_element_type=F32)

    kq = lax.broadcasted_iota(jnp.int32, (blk_q, qw), 0)
    qq = lax.broadcasted_iota(jnp.int32, (blk_q, qw), 1)

    def run_tiles(js, own):
        tiles = [(j, h, c) for j in js for h in range(N_HEADS) for c in range(q_split)]
        pending = {}
        for k in range(len(tiles) + look):
            if k < len(tiles):
                pending[k] = scores(*tiles[k])
            if k < look:
                continue
            j, h, c = tiles[k - look]
            s = pending.pop(k - look)
            cols = slice(c * qw, (c + 1) * qw)
            rows = slice(h * V_ROWS, (h + 1) * V_ROWS)
            if own:
                s = jnp.where(kq <= qq + c * qw, s, NEG)
                m = jnp.max(s, axis=0, keepdims=True)
                m_ref[h:h + 1, cols] = m
                acc_ref[rows, cols] = pv(j, h, jnp.exp2(s - m))
            else:
                picked = sel_ref[h, pl.ds(j, 1), :][:, cols] > 0.0
                m_old = m_ref[h:h + 1, cols]
                m_new = jnp.where(picked, jnp.maximum(m_old, jnp.max(s, axis=0, keepdims=True)), m_old)
                m_ref[h:h + 1, cols] = m_new
                part = pv(j, h, jnp.exp2(s - jnp.where(picked, m_new, BIG)))
                acc_ref[rows, cols] = jnp.exp2(m_old - m_new) * acc_ref[rows, cols] + part

    run_tiles([i], own=True)

    done = 0
    for u in unroll:
        trips = (i - done) // u

        def past_body(jj, c, u=u, done=done):
            run_tiles([done + u * jj + b for b in range(u)], own=False)
            return c

        lax.fori_loop(0, trips, past_body, 0)
        done = done + trips * u

    outs = []
    for h in range(N_HEADS):
        a = acc_ref[h * V_ROWS:(h + 1) * V_ROWS, :]
        outs.append(a[0:HEAD_DIM] / a[HEAD_DIM:HEAD_DIM + 1])
    o_ref[...] = jnp.concatenate(outs, axis=0).T.astype(BF16)


def _prompt_attention(qT, kmeans, k3, v3):
    nb = k3.shape[0]
    t = nb * MOBA_BLOCK
    cfg = _tiles(t)
    return pl.pallas_call(
        functools.partial(_attn_kernel, nb=nb, q_split=cfg["attn_q_split"], look=cfg["attn_look"],
                          unroll=cfg["attn_unroll"]),
        grid=(nb,),
        in_specs=[pl.BlockSpec((ATTN_W, MOBA_BLOCK), lambda i: (0, i)),
                  _resident(kmeans.shape),
                  pl.BlockSpec(memory_space=pl.ANY),
                  pl.BlockSpec(memory_space=pl.ANY)],
        out_specs=pl.BlockSpec((MOBA_BLOCK, ATTN_W), lambda i: (i, 0)),
        out_shape=jax.ShapeDtypeStruct((t, ATTN_W), BF16),
        scratch_shapes=[pltpu.VMEM(k3.shape, BF16), pltpu.VMEM(v3.shape, BF16),
                        pltpu.VMEM((N_HEADS, 2 * HEAD_DIM, MOBA_BLOCK), BF16),
                        pltpu.VMEM((N_HEADS, nb, MOBA_BLOCK), F32),
                        pltpu.VMEM((N_HEADS * V_ROWS, MOBA_BLOCK), F32),
                        pltpu.VMEM((N_HEADS, MOBA_BLOCK), F32),
                        pltpu.SemaphoreType.DMA((2,))],
        compiler_params=pltpu.CompilerParams(dimension_semantics=("arbitrary",),
                                             vmem_limit_bytes=cfg["attn_vmem"]),
        name="moba_prompt",
    )(qT, kmeans, k3, v3)


def _dec_kernel(pt_ref, ksum_ref, qT_ref, kT_ref, vT_ref, ck_hbm, cv_hbm, o_ref,
                idxv_ref, idxs_ref, kg, vg, vlast, sem_g, sem_misc, *, n_pages, seq):
    b = pl.program_id(0)
    n_tok = qT_ref.shape[1]

    def seq_cols(ref, sq, width):
        x = pltpu.roll(ref[...], lax.rem(n_tok - seq * sq, n_tok), axis=1)
        if n_tok < width:
            x = jnp.concatenate([x, jnp.zeros((x.shape[0], width - n_tok), x.dtype)], axis=1)
        return x[:, 0:width]
    n_seq = pl.num_programs(0) - 1
    ppb = MOBA_BLOCK // V7X_LANES
    nblk = n_pages // ppb
    per_q = MOBA_TOPK * ppb
    n_tiles = seq * per_q

    def head_rows(fn, width):
        sub = lax.broadcasted_iota(jnp.int32, (N_HEADS, width), 0)
        out = jnp.broadcast_to(fn(0), (N_HEADS, width))
        for h in range(1, N_HEADS):
            out = jnp.where(sub == h, fn(h), out)
        return out

    def hs(h):
        return slice(h * HEAD_DIM, (h + 1) * HEAD_DIM)

    def tile_copies(st, t, h, page):
        return (pltpu.make_async_copy(ck_hbm.at[page, hs(h)], kg.at[st, t, hs(h)], sem_g.at[st, 0]),
                pltpu.make_async_copy(cv_hbm.at[page, hs(h)], vg.at[st, t, hs(h)], sem_g.at[st, 1]))

    def last_copy(st, page):
        return pltpu.make_async_copy(cv_hbm.at[page], vlast.at[st], sem_misc.at[1 + st])

    @pl.when(b < n_seq)
    def _():
        st = b % 2
        qb = seq_cols(qT_ref, b, V7X_LANES)
        lane_head = lax.broadcasted_iota(jnp.int32, (HEAD_DIM, V7X_LANES), 1) // seq
        qbd = jnp.concatenate(
            [jnp.where(lane_head == h, pltpu.roll(qb[hs(h)], seq * h, axis=1), 0.0) for h in range(N_HEADS)],
            axis=0)
        g = jnp.dot(ksum_ref[0], qbd, precision=HIGHEST, preferred_element_type=F32)
        g = g * (1.0 / MOBA_BLOCK)
        blk = lax.broadcasted_iota(jnp.int32, g.shape, 0)
        sub8 = lax.broadcasted_iota(jnp.int32, (V7X_SUBLANES, V7X_LANES), 0)
        idxv = jnp.zeros((V7X_SUBLANES, V7X_LANES), jnp.int32)
        for r in range(MOBA_TOPK):
            mx = jnp.max(g, axis=0, keepdims=True)
            idx = jnp.min(jnp.where(g == mx, blk, nblk), axis=0, keepdims=True)
            idxv = jnp.where(sub8 == r, idx, idxv)
            g = jnp.where(blk == idx, -jnp.inf, g)
        idxv_ref[...] = idxv
        cp = pltpu.make_async_copy(idxv_ref, idxs_ref, sem_misc.at[0])
        cp.start()
        cp.wait()

        def issue(t, c):
            sr = lax.div(t, ppb)
            pg = t - sr * ppb
            s = lax.div(sr, MOBA_TOPK)
            r = sr - s * MOBA_TOPK
            for h in range(N_HEADS):
                ck, cv = tile_copies(st, t, h, pt_ref[b, idxs_ref[r, h * seq + s] * ppb + pg])
                ck.start()
                cv.start()
            return c

        lax.fori_loop(0, n_tiles, issue, 0)
        last_copy(st, pt_ref[b, n_pages - 1]).start()

    @pl.when(b >= 1)
    def _():
        st = (b - 1) % 2
        for _ in range(n_tiles * N_HEADS):
            ck, cv = tile_copies(st, 0, 0, 0)
            ck.wait()
            cv.wait()
        last_copy(st, 0).wait()

        q_new, k_new, v_new = (seq_cols(ref, b - 1, seq) for ref in (qT_ref, kT_ref, vT_ref))
        lane_s = lax.broadcasted_iota(jnp.int32, (N_HEADS, seq), 1)
        cols = []
        for s in range(seq):
            qcol = q_new[:, s:s + 1] * SM_SCALE

            def tile_scores(t, s=s, qcol=qcol):
                return head_rows(
                    lambda h: jnp.sum(kg[st, s * per_q + t, hs(h), :] * qcol[hs(h)], axis=0, keepdims=True),
                    V7X_LANES)

            scs = [tile_scores(t) for t in range(per_q)]
            loc = head_rows(lambda h: jnp.sum(k_new[hs(h)] * qcol[hs(h)], axis=0, keepdims=True), seq)
            loc = jnp.where(lane_s <= s, loc, NEG)
            mt = scs[0]
            for sc in scs[1:]:
                mt = jnp.maximum(mt, sc)
            m = jnp.maximum(jnp.max(mt, axis=1, keepdims=True), jnp.max(loc, axis=1, keepdims=True))
            ps = [jnp.exp(sc - m) for sc in scs]
            ploc = jnp.exp(loc - m)
            plast = jnp.exp(NEG - m)
            psum = ps[0]
            for p_ in ps[1:]:
                psum = psum + p_
            l = jnp.sum(psum, axis=1, keepdims=True) + jnp.sum(ploc, axis=1, keepdims=True) + V7X_LANES * plast
            rl = 1.0 / l
            outs = []
            for h in range(N_HEADS):
                acc = vlast[st, hs(h), :] * plast[h:h + 1, :]
                for t in range(per_q):
                    acc = acc + vg[st, s * per_q + t, hs(h), :] * ps[t][h:h + 1, :]
                o = jnp.sum(acc, axis=1, keepdims=True)
                o = o + jnp.sum(v_new[hs(h)] * ploc[h:h + 1, :], axis=1, keepdims=True)
                outs.append(o * rl[h:h + 1, :])
            cols.append(jnp.concatenate(outs, axis=0))
        o_ref[0] = jnp.concatenate(cols, axis=1)


def _decode_attention(page_table, qT, kT, vT, ksum, ckT, cvT, *, seq):
    bsz = ksum.shape[0]
    n_pages = page_table.shape[1]
    ppb = MOBA_BLOCK // V7X_LANES
    nblk = n_pages // ppb
    assert ckT.shape[1:] == (ATTN_W, V7X_LANES) and n_pages % ppb == 0 and ksum.shape == (bsz, nblk, ATTN_W)
    assert MOBA_TOPK <= V7X_SUBLANES and N_HEADS * seq <= V7X_LANES and qT.shape == (ATTN_W, bsz * seq)
    n_tiles = seq * MOBA_TOPK * ppb
    cfg = _tiles(bsz * seq)
    tok = _resident(qT.shape)
    grid_spec = pltpu.PrefetchScalarGridSpec(
        num_scalar_prefetch=1, grid=(bsz + 1,),
        in_specs=[pl.BlockSpec((1, nblk, ATTN_W), lambda b, pt: (jnp.minimum(b, bsz - 1), 0, 0)),
                  tok, tok, tok,
                  pl.BlockSpec(memory_space=pl.ANY), pl.BlockSpec(memory_space=pl.ANY)],
        out_specs=pl.BlockSpec((1, ATTN_W, seq), lambda b, pt: (jnp.maximum(b - 1, 0), 0, 0)),
        scratch_shapes=[pltpu.VMEM((N_HEADS, V7X_LANES), jnp.int32),
                        pltpu.SMEM((N_HEADS, V7X_LANES), jnp.int32),
                        pltpu.VMEM((2, n_tiles, ATTN_W, V7X_LANES), F32),
                        pltpu.VMEM((2, n_tiles, ATTN_W, V7X_LANES), F32),
                        pltpu.VMEM((2, ATTN_W, V7X_LANES), F32),
                        pltpu.SemaphoreType.DMA((2, 2)),
                        pltpu.SemaphoreType.DMA((3,))])
    return pl.pallas_call(
        functools.partial(_dec_kernel, n_pages=n_pages, seq=seq),
        grid_spec=grid_spec,
        out_shape=jax.ShapeDtypeStruct((bsz, ATTN_W, seq), F32),
        compiler_params=pltpu.CompilerParams(dimension_semantics=("arbitrary",),
                                             vmem_limit_bytes=cfg["dec_vmem"]),
        name="moba_decode",
    )(page_table, ksum, qT, kT, vT, ckT, cvT)


def _sc_block_sums(page_table, ckT):
    bsz, n_pages = page_table.shape
    n_pool = ckT.shape[0]
    ppb = MOBA_BLOCK // V7X_LANES
    nblk = n_pages // ppb
    n_workers = V7X_SC_CORES * V7X_SC_SUBCORES
    assert bsz % n_workers == 0 and ATTN_W % SC_PIECE_ROWS == 0
    seq_per_worker = bsz // n_workers
    pieces = ATTN_W // SC_PIECE_ROWS
    steps = nblk * pieces
    table = ckT.reshape(n_pool * pieces, SC_PIECE_ROWS, V7X_LANES)
    idx = (page_table.reshape(bsz, nblk, 1, ppb) * pieces
           + jnp.arange(pieces, dtype=jnp.int32).reshape(1, 1, pieces, 1)).reshape(-1)
    lanes_per_row = V7X_LANES // V7X_SC_LANES

    def body(idx_hbm, tab_hbm, out_hbm, idx_v, buf, orow, sem):
        wid = lax.axis_index("c") * V7X_SC_SUBCORES + lax.axis_index("s")
        lane = lax.iota(jnp.int32, V7X_SC_LANES)

        def fetch(n, slot):
            return pltpu.make_async_copy(tab_hbm.at[idx_v.at[pl.ds(n * ppb, ppb)]], buf.at[slot], sem.at[slot])

        for q in range(seq_per_worker):
            sq = wid * seq_per_worker + q
            pltpu.sync_copy(idx_hbm.at[pl.ds(sq * (steps * ppb), steps * ppb)], idx_v)
            fetch(0, 0).start()

            @pl.loop(0, steps, step=2)
            def _(n0):
                for slot in range(2):
                    n = n0 + slot

                    @pl.when(n + 1 < steps)
                    def _():
                        fetch(n + 1, 1 - slot).start()

                    fetch(n, slot).wait()

                    @pl.loop(0, SC_PIECE_ROWS, step=V7X_SC_LANES)
                    def _(r0):
                        vec = jnp.zeros((V7X_SC_LANES,), F32)
                        for i in range(V7X_SC_LANES):
                            acc = None
                            for pg in range(ppb):
                                for k in range(lanes_per_row):
                                    v = buf[slot, pg, r0 + i, pl.ds(k * V7X_SC_LANES, V7X_SC_LANES)]
                                    acc = v if acc is None else acc + v
                            vec = jnp.where(lane == i, jnp.sum(acc), vec)
                        orow[pl.ds(r0, V7X_SC_LANES)] = vec

                    pltpu.sync_copy(orow, out_hbm.at[pl.ds((sq * steps + n) * SC_PIECE_ROWS, SC_PIECE_ROWS)])

    sums = pl.kernel(
        body,
        out_type=jax.ShapeDtypeStruct((bsz * steps * SC_PIECE_ROWS,), F32),
        mesh=plsc.VectorSubcoreMesh(core_axis_name="c", subcore_axis_name="s"),
        scratch_types=[pltpu.VMEM((steps * ppb,), jnp.int32),
                       pltpu.VMEM((2, ppb, SC_PIECE_ROWS, V7X_LANES), F32),
                       pltpu.VMEM((SC_PIECE_ROWS,), F32),
                       pltpu.SemaphoreType.DMA((2,))],
        compiler_params=pltpu.CompilerParams(needs_layout_passes=False),
        name="sc_block_sums",
    )(idx, table)
    return sums.reshape(bsz, nblk, ATTN_W)


def _mlp_kernel(x_ref, a_ref, y_ref, sga_ref, sgc_ref, wba_ref, wbc_ref, wo_ref, g_ref, wup_ref, wdn_ref,
                o_ref, *, chunk):
    a = jnp.dot(a_ref[...], wba_ref[...], preferred_element_type=F32)
    c = jnp.dot(y_ref[...], wbc_ref[...], preferred_element_type=F32)
    mixed = sga_ref[...].astype(F32) * a + sgc_ref[...].astype(F32) * c
    x1 = x_ref[...] + jnp.dot(mixed.astype(BF16), wo_ref[...], preferred_element_type=F32)
    h2 = _rms_rows(x1, g_ref[...]).astype(BF16)
    acc = x1
    for c0 in range(0, wup_ref.shape[1], chunk):
        hid = jnp.maximum(jnp.dot(h2, wup_ref[:, c0:c0 + chunk], preferred_element_type=F32), 0.0)
        acc = acc + jnp.dot((hid * hid).astype(BF16), wdn_ref[c0:c0 + chunk, :], preferred_element_type=F32)
    o_ref[...] = acc


def _merge_mlp(x2d, attn, y, sga, sgc, wba, wbc, wo, g_mlp, wup, wdn):
    n, d = x2d.shape
    cfg = _tiles(n)
    tm = cfg["tm"]
    row = lambda w: pl.BlockSpec((tm, w), lambda i: (i, 0))
    weights = [wba, wbc, wo, g_mlp, wup, wdn]
    return pl.pallas_call(
        functools.partial(_mlp_kernel, chunk=cfg["mlp_chunk"]),
        grid=(n // tm,),
        in_specs=[row(d), row(attn.shape[1]), row(d), row(d), row(d)] + [_resident(w.shape) for w in weights],
        out_specs=row(d),
        out_shape=jax.ShapeDtypeStruct((n, d), F32),
        compiler_params=pltpu.CompilerParams(dimension_semantics=("arbitrary",),
                                             vmem_limit_bytes=cfg["mlp_vmem"]),
        name="merge_mlp",
    )(x2d, attn, y, sga, sgc, *weights)


def _heads_last(xT, lead):
    n = xT.shape[1]
    return jnp.transpose(xT.reshape(N_HEADS, HEAD_DIM, n), (2, 0, 1)).reshape(*lead, N_HEADS, HEAD_DIM)


def kernel(x_prompt, x_sample, cache_k, cache_v, state_conv, page_table, norm_mix, w_in, q_norm, k_norm, conv_w, w_br_attn, w_br_conv, w_out, norm_mlp, w_up, w_down):
    bp, t, d = x_prompt.shape
    db, s, _ = x_sample.shape
    depth = w_in.shape[0]
    n_pool, page = cache_k.shape[1], cache_k.shape[2]
    assert bp == 1 and page == V7X_LANES and t % MOBA_BLOCK == 0
    past = page_table.shape[1] * page
    inv = (ROPE_THETA ** (-jnp.arange(ROT_HALF, dtype=F32) * 2.0 / ROT_DIM)).reshape(ROT_HALF, 1)

    ckT_l = [jnp.transpose(cache_k[l], (0, 2, 3, 1)).reshape(n_pool, ATTN_W, page) for l in range(depth)]
    cvT_l = [jnp.transpose(cache_v[l], (0, 2, 3, 1)).reshape(n_pool, ATTN_W, page) for l in range(depth)]
    ksum = [_sc_block_sums(page_table, ckT_l[l]) for l in range(depth)]

    xp = x_prompt.reshape(t, d)
    xs = x_sample.reshape(db * s, d)
    kp_l, vp_l, cp_l, ks_l, vs_l, cs_l = [], [], [], [], [], []
    for l in range(depth):
        w_qkvT = w_in[l][:, :3 * ATTN_W].T.astype(BF16)
        w_rest = w_in[l][:, 3 * ATTN_W:].astype(BF16)
        g_mix = norm_mix[l].reshape(1, d)
        qg = q_norm[l].reshape(HEAD_DIM, 1)
        kg = k_norm[l].reshape(HEAD_DIM, 1)
        cw = conv_w[l]
        weights = (w_br_attn[l].astype(BF16), w_br_conv[l].astype(BF16), w_out[l].astype(BF16),
                   norm_mlp[l].reshape(1, d), w_up[l].astype(BF16), w_down[l].astype(BF16))

        qT, kT, vT, y, sga, sgc, u_tail, k3, v3, km = _project(xp, g_mix, w_qkvT, w_rest, qg, kg, inv, cw, pos0=0)
        attn = _prompt_attention(qT, km.reshape(t // MOBA_BLOCK, ATTN_W), k3, v3)
        xp = _merge_mlp(xp, attn, y, sga, sgc, *weights)
        kp_l.append(_heads_last(kT, (bp, t)))
        vp_l.append(_heads_last(vT, (bp, t)))
        cp_l.append(u_tail[V7X_SUBLANES - (CONV_K - 1):].reshape(bp, CONV_K - 1, d))

        st = state_conv[l]
        zero = jnp.zeros((db, s, d), F32)
        s1 = zero.at[:, 0].set(st[:, 1]).reshape(db * s, d)
        s2 = zero.at[:, 0].set(st[:, 0]).at[:, 1].set(st[:, 1]).reshape(db * s, d)
        qT, kT, vT, y, sga, sgc, u = _project(xs, g_mix, w_qkvT, w_rest, qg, kg, inv, cw, pos0=past,
                                               prev=(s1, s2, s))
        oT = _decode_attention(page_table, qT, kT, vT, ksum[l], ckT_l[l], cvT_l[l], seq=s)
        attn = jnp.transpose(oT, (0, 2, 1)).reshape(db * s, ATTN_W).astype(BF16)
        xs = _merge_mlp(xs, attn, y, sga, sgc, *weights)
        ks_l.append(_heads_last(kT, (db, s)))
        vs_l.append(_heads_last(vT, (db, s)))
        cs_l.append(u.reshape(db, s, d)[:, s - (CONV_K - 1):])

    return (xp.reshape(bp, t, d), xs.reshape(db, s, d), jnp.stack(kp_l), jnp.stack(vp_l), jnp.stack(cp_l),
            jnp.stack(ks_l), jnp.stack(vs_l), jnp.stack(cs_l))
```

```python
import functools

import jax
import jax.numpy as jnp
from jax import lax
from jax.experimental import pallas as pl
from jax.experimental.pallas import tpu as pltpu
from jax.experimental.pallas import tpu_sc as plsc

N_HEADS = 8
HEAD_DIM = 64
ATTN_W = N_HEADS * HEAD_DIM
MOBA_BLOCK = 256
MOBA_TOPK = 3
ROT_DIM = HEAD_DIM // 4
ROT_HALF = ROT_DIM // 2
ROPE_THETA = 500000.0
EPS = 1e-6
NEG = -1e30
CONV_K = 3
SM_SCALE = HEAD_DIM ** -0.5
LOG2E = 1.4426950408889634
BIG = 1e30
V_ROWS = HEAD_DIM + 16

V7X_SUBLANES = 8
V7X_LANES = 128
MIB = 1024 * 1024
V7X_SC_CORES = 2
V7X_SC_SUBCORES = 16
V7X_SC_LANES = 16
SC_PIECE_ROWS = 128

F32 = jnp.float32
BF16 = jnp.bfloat16
HIGHEST = lax.Precision.HIGHEST


def _tiles(n_rows):
    tm = 512 if n_rows % 512 == 0 else n_rows
    return dict(tm=tm, proj_vmem=56 * MIB, attn_vmem=52 * MIB, mlp_vmem=56 * MIB, dec_vmem=36 * MIB,
                attn_look=6, attn_unroll=(16, 8, 4, 1), proj_cols=256, mlp_chunk=1024)


def _resident(shape):
    nd = len(shape)
    return pl.BlockSpec(shape, lambda *_: (0,) * nd, pipeline_mode=pl.Buffered(1))


def _rms_rows(x, g):
    ms = jnp.mean(x * x, axis=-1, keepdims=True)
    return x * lax.rsqrt(ms + EPS) * g


def _sigmoid(x):
    return 1.0 / (1.0 + jnp.exp(-x))


def _proj_kernel(*refs, tm, pos0, pos_mask, prompt, cols):
    if prompt:
        (x_ref, g_ref, wt_ref, wr_ref, qg_ref, kg_ref, inv_ref, cw_ref,
         qT_ref, kT_ref, vT_ref, y_ref, sga_ref, sgc_ref, u_ref,
         k3_ref, v3_ref, km_ref, carry_ref) = refs
    else:
        (x_ref, g_ref, wt_ref, wr_ref, qg_ref, kg_ref, inv_ref, cw_ref, s1_ref, s2_ref,
         qT_ref, kT_ref, vT_ref, y_ref, sga_ref, sgc_ref, u_ref) = refs
    i = pl.program_id(0)
    width = x_ref.shape[1]

    h = _rms_rows(x_ref[...], g_ref[...]).astype(BF16)

    zT = lax.dot_general(wt_ref[...], h, (((1,), (1,)), ((), ())), preferred_element_type=F32)

    col = lax.broadcasted_iota(jnp.int32, (1, tm), 1) + i * tm
    if pos_mask is not None:
        col = col & pos_mask
    ang = inv_ref[...] * (col + pos0).astype(F32)
    cos = jnp.cos(ang)
    sin = jnp.sin(ang)

    def norm_rope(zt, gain):
        outs = []
        for hh in range(N_HEADS):
            blk = zt[hh * HEAD_DIM:(hh + 1) * HEAD_DIM]
            ms = jnp.mean(blk * blk, axis=0, keepdims=True)
            blk = blk * lax.rsqrt(ms + EPS) * gain
            x1 = blk[0:ROT_HALF]
            x2 = blk[ROT_HALF:ROT_DIM]
            outs += [x1 * cos - x2 * sin, x2 * cos + x1 * sin, blk[ROT_DIM:HEAD_DIM]]
        return jnp.concatenate(outs, axis=0)

    qT = norm_rope(zT[0:ATTN_W], qg_ref[...])
    kT = norm_rope(zT[ATTN_W:2 * ATTN_W], kg_ref[...])
    vT = zT[2 * ATTN_W:3 * ATTN_W]
    qT_ref[...] = qT
    kT_ref[...] = kT
    vT_ref[...] = vT

    if prompt:
        k_rows = kT.T
        pad_row = lax.broadcasted_iota(jnp.int32, (V_ROWS - HEAD_DIM, MOBA_BLOCK), 0)
        ones_pad = jnp.where(pad_row == 0, 1.0, 0.0).astype(BF16)
        for g in range(tm // MOBA_BLOCK):
            kb = k_rows[g * MOBA_BLOCK:(g + 1) * MOBA_BLOCK]
            km_ref[0, g:g + 1, :] = jnp.mean(kb, axis=0, keepdims=True)
            k3_ref[g] = kb.astype(BF16)
            vb = vT[:, g * MOBA_BLOCK:(g + 1) * MOBA_BLOCK].astype(BF16)
            pieces = []
            for hh in range(N_HEADS):
                pieces += [vb[hh * HEAD_DIM:(hh + 1) * HEAD_DIM], ones_pad]
            v3_ref[g] = jnp.concatenate(pieces, axis=0)

    if prompt:
        @pl.when(i == 0)
        def _():
            carry_ref[...] = jnp.zeros_like(carry_ref)
    else:
        rm = lax.broadcasted_iota(jnp.int32, (tm, 1), 0) & pos_mask
    for c0 in range(0, width, cols):
        cs = slice(c0, c0 + cols)

        def seg(k, c0=c0):
            return jnp.dot(h, wr_ref[:, k * width + c0:k * width + c0 + cols], preferred_element_type=F32)

        u = seg(2) * seg(0)
        carry = carry_ref[:, cs] if prompt else jnp.zeros((V7X_SUBLANES, cols), F32)
        up = jnp.concatenate([carry, u], axis=0)
        u1 = up[V7X_SUBLANES - 1:V7X_SUBLANES - 1 + tm]
        u2 = up[V7X_SUBLANES - 2:V7X_SUBLANES - 2 + tm]
        if not prompt:
            u1 = jnp.where(rm >= 1, u1, s1_ref[:, cs])
            u2 = jnp.where(rm >= 2, u2, s2_ref[:, cs])
        conv = cw_ref[0:1, cs] * u2 + cw_ref[1:2, cs] * u1 + cw_ref[2:3, cs] * u
        y_ref[:, cs] = (seg(1) * conv).astype(BF16)
        sga_ref[:, cs] = _sigmoid(seg(3)).astype(BF16)
        sgc_ref[:, cs] = _sigmoid(seg(4)).astype(BF16)
        if prompt:
            tail = u[tm - V7X_SUBLANES:tm]
            carry_ref[:, cs] = tail
            u_ref[:, cs] = tail
        else:
            u_ref[:, cs] = u


def _project(x2d, g_mix, w_qkvT, w_rest, qg, kg, inv, cw, *, pos0, prev=None):
    n, d = x2d.shape
    cfg = _tiles(n)
    tm = cfg["tm"]
    prompt = prev is None
    nt = n // tm
    row = lambda w: pl.BlockSpec((tm, w), lambda i: (i, 0))
    colT = pl.BlockSpec((ATTN_W, tm), lambda i: (0, i))
    in_specs = [row(d), _resident(g_mix.shape), _resident(w_qkvT.shape), _resident(w_rest.shape),
                _resident(qg.shape), _resident(kg.shape), _resident(inv.shape), _resident(cw.shape)]
    args = [x2d, g_mix, w_qkvT, w_rest, qg, kg, inv, cw]
    out_shape = [jax.ShapeDtypeStruct((ATTN_W, n), F32)] * 3 + [jax.ShapeDtypeStruct((n, d), BF16)] * 3
    out_specs = [colT, colT, colT, row(d), row(d), row(d)]
    scratch = []
    if prompt:
        assert tm % MOBA_BLOCK == 0
        gpt = tm // MOBA_BLOCK
        out_shape += [jax.ShapeDtypeStruct((V7X_SUBLANES, d), F32),
                      jax.ShapeDtypeStruct((n // MOBA_BLOCK, MOBA_BLOCK, ATTN_W), BF16),
                      jax.ShapeDtypeStruct((n // MOBA_BLOCK, N_HEADS * V_ROWS, MOBA_BLOCK), BF16),
                      jax.ShapeDtypeStruct((nt, gpt, ATTN_W), F32)]
        out_specs += [pl.BlockSpec((V7X_SUBLANES, d), lambda i: (0, 0)),
                      pl.BlockSpec((gpt, MOBA_BLOCK, ATTN_W), lambda i: (i, 0, 0)),
                      pl.BlockSpec((gpt, N_HEADS * V_ROWS, MOBA_BLOCK), lambda i: (i, 0, 0)),
                      pl.BlockSpec((1, gpt, ATTN_W), lambda i: (i, 0, 0))]
        scratch = [pltpu.VMEM((V7X_SUBLANES, d), F32)]
        pos_mask = None
    else:
        s1, s2, seq = prev
        assert seq & (seq - 1) == 0
        pos_mask = seq - 1
        in_specs += [row(d), row(d)]
        args += [s1, s2]
        out_shape += [jax.ShapeDtypeStruct((n, d), F32)]
        out_specs += [row(d)]
    return pl.pallas_call(
        functools.partial(_proj_kernel, tm=tm, pos0=pos0, pos_mask=pos_mask, prompt=prompt,
                          cols=cfg["proj_cols"]),
        grid=(nt,), in_specs=in_specs, out_specs=out_specs, out_shape=out_shape, scratch_shapes=scratch,
        compiler_params=pltpu.CompilerParams(dimension_semantics=("arbitrary",),
                                             vmem_limit_bytes=cfg["proj_vmem"]),
        name="proj_prompt" if prompt else "proj_sample",
    )(*args)


def _attn_kernel(q_ref, km_ref, k_hbm, v_hbm, o_ref,
                 kbuf, vbuf, qm_ref, sel_ref, acc_ref, m_ref, sem, *, nb, look, unroll):
    i = pl.program_id(0)
    blk_q = MOBA_BLOCK

    @pl.when(i == 0)
    def _():
        ck = pltpu.make_async_copy(k_hbm, kbuf, sem.at[0])
        cv = pltpu.make_async_copy(v_hbm, vbuf, sem.at[1])
        ck.start()
        cv.start()
        ck.wait()
        cv.wait()

    pair_w = 2 * HEAD_DIM
    blk = lax.broadcasted_iota(jnp.int32, (nb, blk_q), 0)
    rid = lax.broadcasted_iota(jnp.int32, (pair_w, 1), 0)
    for h in range(N_HEADS):
        pr = h // 2
        qp = q_ref[pr * pair_w:(pr + 1) * pair_w, :]
        keep = (rid >= HEAD_DIM) if h % 2 else (rid < HEAD_DIM)
        qm = jnp.where(keep, qp, 0.0)
        gate = jnp.dot(km_ref[:, pr * pair_w:(pr + 1) * pair_w], qm,
                       precision=HIGHEST, preferred_element_type=F32)
        g = jnp.where(blk < i, gate, -jnp.inf)
        sel = jnp.zeros((nb, blk_q), F32)
        for r in range(MOBA_TOPK):
            mx = jnp.max(g, axis=0, keepdims=True)
            idx = jnp.min(jnp.where(g == mx, blk, nb), axis=0, keepdims=True)
            hit = blk == idx
            sel = jnp.where(jnp.logical_and(hit, i > r), 1.0, sel)
            g = jnp.where(hit, -jnp.inf, g)
        sel_ref[h] = sel
        qm_ref[h] = (qm * (SM_SCALE * LOG2E)).astype(BF16)

    def scores(j, h):
        pr = h // 2
        return jnp.dot(kbuf[j, :, pr * pair_w:(pr + 1) * pair_w], qm_ref[h],
                       preferred_element_type=F32)

    def pv(j, h, p):
        return jnp.dot(vbuf[j, h * V_ROWS:(h + 1) * V_ROWS, :], p.astype(BF16), preferred_element_type=F32)

    kq = lax.broadcasted_iota(jnp.int32, (blk_q, blk_q), 0)
    qq = lax.broadcasted_iota(jnp.int32, (blk_q, blk_q), 1)

    def run_tiles(js, own):
        tiles = [(j, h) for j in js for h in range(N_HEADS)]
        pending = {}
        for k in range(len(tiles) + look):
            if k < len(tiles):
                pending[k] = scores(*tiles[k])
            if k < look:
                continue
            j, h = tiles[k - look]
            s = pending.pop(k - look)
            rows = slice(h * V_ROWS, (h + 1) * V_ROWS)
            if own:
                s = jnp.where(kq <= qq, s, NEG)
                m = jnp.max(s, axis=0, keepdims=True)
                m_ref[h:h + 1, :] = m
                acc_ref[rows, :] = pv(j, h, jnp.exp2(s - m))
            else:
                picked = sel_ref[h, pl.ds(j, 1), :] > 0.0
                m_old = m_ref[h:h + 1, :]
                m_new = jnp.where(picked, jnp.maximum(m_old, jnp.max(s, axis=0, keepdims=True)), m_old)
                m_ref[h:h + 1, :] = m_new
                part = pv(j, h, jnp.exp2(s - jnp.where(picked, m_new, BIG)))
                acc_ref[rows, :] = jnp.exp2(m_old - m_new) * acc_ref[rows, :] + part

    run_tiles([i], own=True)

    done = 0
    for u in unroll:
        trips = (i - done) // u

        def past_body(jj, c, u=u, done=done):
            run_tiles([done + u * jj + b for b in range(u)], own=False)
            return c

        lax.fori_loop(0, trips, past_body, 0)
        done = done + trips * u

    outs = []
    for h in range(N_HEADS):
        a = acc_ref[h * V_ROWS:(h + 1) * V_ROWS, :]
        outs.append(a[0:HEAD_DIM] / a[HEAD_DIM:HEAD_DIM + 1])
    o_ref[...] = jnp.concatenate(outs, axis=0).T.astype(BF16)


def _prompt_attention(qT, kmeans, k3, v3):
    nb = k3.shape[0]
    t = nb * MOBA_BLOCK
    cfg = _tiles(t)
    return pl.pallas_call(
        functools.partial(_attn_kernel, nb=nb, look=cfg["attn_look"],
                          unroll=cfg["attn_unroll"]),
        grid=(nb,),
        in_specs=[pl.BlockSpec((ATTN_W, MOBA_BLOCK), lambda i: (0, i)),
                  _resident(kmeans.shape),
                  pl.BlockSpec(memory_space=pl.ANY),
                  pl.BlockSpec(memory_space=pl.ANY)],
        out_specs=pl.BlockSpec((MOBA_BLOCK, ATTN_W), lambda i: (i, 0)),
        out_shape=jax.ShapeDtypeStruct((t, ATTN_W), BF16),
        scratch_shapes=[pltpu.VMEM(k3.shape, BF16), pltpu.VMEM(v3.shape, BF16),
                        pltpu.VMEM((N_HEADS, 2 * HEAD_DIM, MOBA_BLOCK), BF16),
                        pltpu.VMEM((N_HEADS, nb, MOBA_BLOCK), F32),
                        pltpu.VMEM((N_HEADS * V_ROWS, MOBA_BLOCK), F32),
                        pltpu.VMEM((N_HEADS, MOBA_BLOCK), F32),
                        pltpu.SemaphoreType.DMA((2,))],
        compiler_params=pltpu.CompilerParams(dimension_semantics=("arbitrary",),
                                             vmem_limit_bytes=cfg["attn_vmem"]),
        name="moba_prompt",
    )(qT, kmeans, k3, v3)


def _dec_kernel(pt_ref, ksum_ref, qT_ref, kT_ref, vT_ref, ck_hbm, cv_hbm, o_ref,
                idxv_ref, idxs_ref, kg, vg, vlast, sem_g, sem_misc, *, n_pages, seq):
    b = pl.program_id(0)
    n_tok = qT_ref.shape[1]

    def seq_cols(ref, sq, width):
        x = pltpu.roll(ref[...], lax.rem(n_tok - seq * sq, n_tok), axis=1)
        if n_tok < width:
            x = jnp.concatenate([x, jnp.zeros((x.shape[0], width - n_tok), x.dtype)], axis=1)
        return x[:, 0:width]
    n_seq = pl.num_programs(0) - 1
    ppb = MOBA_BLOCK // V7X_LANES
    nblk = n_pages // ppb
    per_q = MOBA_TOPK * ppb
    n_tiles = seq * per_q

    def head_rows(fn, width):
        sub = lax.broadcasted_iota(jnp.int32, (N_HEADS, width), 0)
        out = jnp.broadcast_to(fn(0), (N_HEADS, width))
        for h in range(1, N_HEADS):
            out = jnp.where(sub == h, fn(h), out)
        return out

    def hs(h):
        return slice(h * HEAD_DIM, (h + 1) * HEAD_DIM)

    def tile_copies(st, t, h, page):
        return (pltpu.make_async_copy(ck_hbm.at[page, hs(h)], kg.at[st, t, hs(h)], sem_g.at[st, 0]),
                pltpu.make_async_copy(cv_hbm.at[page, hs(h)], vg.at[st, t, hs(h)], sem_g.at[st, 1]))

    def last_copy(st, page):
        return pltpu.make_async_copy(cv_hbm.at[page], vlast.at[st], sem_misc.at[1 + st])

    @pl.when(b < n_seq)
    def _():
        st = b % 2
        qb = seq_cols(qT_ref, b, V7X_LANES)
        lane_head = lax.broadcasted_iota(jnp.int32, (HEAD_DIM, V7X_LANES), 1) // seq
        qbd = jnp.concatenate(
            [jnp.where(lane_head == h, pltpu.roll(qb[hs(h)], seq * h, axis=1), 0.0) for h in range(N_HEADS)],
            axis=0)
        g = jnp.dot(ksum_ref[0], qbd, precision=HIGHEST, preferred_element_type=F32)
        g = g * (1.0 / MOBA_BLOCK)
        blk = lax.broadcasted_iota(jnp.int32, g.shape, 0)
        sub8 = lax.broadcasted_iota(jnp.int32, (V7X_SUBLANES, V7X_LANES), 0)
        idxv = jnp.zeros((V7X_SUBLANES, V7X_LANES), jnp.int32)
        for r in range(MOBA_TOPK):
            mx = jnp.max(g, axis=0, keepdims=True)
            idx = jnp.min(jnp.where(g == mx, blk, nblk), axis=0, keepdims=True)
            idxv = jnp.where(sub8 == r, idx, idxv)
            g = jnp.where(blk == idx, -jnp.inf, g)
        idxv_ref[...] = idxv
        cp = pltpu.make_async_copy(idxv_ref, idxs_ref, sem_misc.at[0])
        cp.start()
        cp.wait()

        def issue(t, c):
            sr = lax.div(t, ppb)
            pg = t - sr * ppb
            s = lax.div(sr, MOBA_TOPK)
            r = sr - s * MOBA_TOPK
            for h in range(N_HEADS):
                ck, cv = tile_copies(st, t, h, pt_ref[b, idxs_ref[r, h * seq + s] * ppb + pg])
                ck.start()
                cv.start()
            return c

        lax.fori_loop(0, n_tiles, issue, 0)
        last_copy(st, pt_ref[b, n_pages - 1]).start()

    @pl.when(b >= 1)
    def _():
        st = (b - 1) % 2
        for _ in range(n_tiles * N_HEADS):
            ck, cv = tile_copies(st, 0, 0, 0)
            ck.wait()
            cv.wait()
        last_copy(st, 0).wait()

        q_new, k_new, v_new = (seq_cols(ref, b - 1, seq) for ref in (qT_ref, kT_ref, vT_ref))
        lane_s = lax.broadcasted_iota(jnp.int32, (N_HEADS, seq), 1)
        cols = []
        for s in range(seq):
            qcol = q_new[:, s:s + 1] * SM_SCALE

            def tile_scores(t, s=s, qcol=qcol):
                return head_rows(
                    lambda h: jnp.sum(kg[st, s * per_q + t, hs(h), :] * qcol[hs(h)], axis=0, keepdims=True),
                    V7X_LANES)

            scs = [tile_scores(t) for t in range(per_q)]
            loc = head_rows(lambda h: jnp.sum(k_new[hs(h)] * qcol[hs(h)], axis=0, keepdims=True), seq)
            loc = jnp.where(lane_s <= s, loc, NEG)
            mt = scs[0]
            for sc in scs[1:]:
                mt = jnp.maximum(mt, sc)
            m = jnp.maximum(jnp.max(mt, axis=1, keepdims=True), jnp.max(loc, axis=1, keepdims=True))
            ps = [jnp.exp(sc - m) for sc in scs]
            ploc = jnp.exp(loc - m)
            plast = jnp.exp(NEG - m)
            psum = ps[0]
            for p_ in ps[1:]:
                psum = psum + p_
            l = jnp.sum(psum, axis=1, keepdims=True) + jnp.sum(ploc, axis=1, keepdims=True) + V7X_LANES * plast
            rl = 1.0 / l
            outs = []
            for h in range(N_HEADS):
                acc = vlast[st, hs(h), :] * plast[h:h + 1, :]
                for t in range(per_q):
                    acc = acc + vg[st, s * per_q + t, hs(h), :] * ps[t][h:h + 1, :]
                o = jnp.sum(acc, axis=1, keepdims=True)
                o = o + jnp.sum(v_new[hs(h)] * ploc[h:h + 1, :], axis=1, keepdims=True)
                outs.append(o * rl[h:h + 1, :])
            cols.append(jnp.concatenate(outs, axis=0))
        o_ref[0] = jnp.concatenate(cols, axis=1)


def _decode_attention(page_table, qT, kT, vT, ksum, ckT, cvT, *, seq):
    bsz = ksum.shape[0]
    n_pages = page_table.shape[1]
    ppb = MOBA_BLOCK // V7X_LANES
    nblk = n_pages // ppb
    assert ckT.shape[1:] == (ATTN_W, V7X_LANES) and n_pages % ppb == 0 and ksum.shape == (bsz, nblk, ATTN_W)
    assert MOBA_TOPK <= V7X_SUBLANES and N_HEADS * seq <= V7X_LANES and qT.shape == (ATTN_W, bsz * seq)
    n_tiles = seq * MOBA_TOPK * ppb
    cfg = _tiles(bsz * seq)
    tok = _resident(qT.shape)
    grid_spec = pltpu.PrefetchScalarGridSpec(
        num_scalar_prefetch=1, grid=(bsz + 1,),
        in_specs=[pl.BlockSpec((1, nblk, ATTN_W), lambda b, pt: (jnp.minimum(b, bsz - 1), 0, 0)),
                  tok, tok, tok,
                  pl.BlockSpec(memory_space=pl.ANY), pl.BlockSpec(memory_space=pl.ANY)],
        out_specs=pl.BlockSpec((1, ATTN_W, seq), lambda b, pt: (jnp.maximum(b - 1, 0), 0, 0)),
        scratch_shapes=[pltpu.VMEM((N_HEADS, V7X_LANES), jnp.int32),
                        pltpu.SMEM((N_HEADS, V7X_LANES), jnp.int32),
                        pltpu.VMEM((2, n_tiles, ATTN_W, V7X_LANES), F32),
                        pltpu.VMEM((2, n_tiles, ATTN_W, V7X_LANES), F32),
                        pltpu.VMEM((2, ATTN_W, V7X_LANES), F32),
                        pltpu.SemaphoreType.DMA((2, 2)),
                        pltpu.SemaphoreType.DMA((3,))])
    return pl.pallas_call(
        functools.partial(_dec_kernel, n_pages=n_pages, seq=seq),
        grid_spec=grid_spec,
        out_shape=jax.ShapeDtypeStruct((bsz, ATTN_W, seq), F32),
        compiler_params=pltpu.CompilerParams(dimension_semantics=("arbitrary",),
                                             vmem_limit_bytes=cfg["dec_vmem"]),
        name="moba_decode",
    )(page_table, ksum, qT, kT, vT, ckT, cvT)


def _sc_block_sums(page_table, ckT):
    bsz, n_pages = page_table.shape
    n_pool = ckT.shape[0]
    ppb = MOBA_BLOCK // V7X_LANES
    nblk = n_pages // ppb
    n_workers = V7X_SC_CORES * V7X_SC_SUBCORES
    assert bsz % n_workers == 0 and ATTN_W % SC_PIECE_ROWS == 0
    seq_per_worker = bsz // n_workers
    pieces = ATTN_W // SC_PIECE_ROWS
    steps = nblk * pieces
    table = ckT.reshape(n_pool * pieces, SC_PIECE_ROWS, V7X_LANES)
    idx = (page_table.reshape(bsz, nblk, 1, ppb) * pieces
           + jnp.arange(pieces, dtype=jnp.int32).reshape(1, 1, pieces, 1)).reshape(-1)
    lanes_per_row = V7X_LANES // V7X_SC_LANES

    def body(idx_hbm, tab_hbm, out_hbm, idx_v, buf, orow, sem):
        wid = lax.axis_index("c") * V7X_SC_SUBCORES + lax.axis_index("s")
        lane = lax.iota(jnp.int32, V7X_SC_LANES)

        def fetch(n, slot):
            return pltpu.make_async_copy(tab_hbm.at[idx_v.at[pl.ds(n * ppb, ppb)]], buf.at[slot], sem.at[slot])

        for q in range(seq_per_worker):
            sq = wid * seq_per_worker + q
            pltpu.sync_copy(idx_hbm.at[pl.ds(sq * (steps * ppb), steps * ppb)], idx_v)
            fetch(0, 0).start()

            @pl.loop(0, steps, step=2)
            def _(n0):
                for slot in range(2):
                    n = n0 + slot

                    @pl.when(n + 1 < steps)
                    def _():
                        fetch(n + 1, 1 - slot).start()

                    fetch(n, slot).wait()

                    @pl.loop(0, SC_PIECE_ROWS, step=V7X_SC_LANES)
                    def _(r0):
                        vec = jnp.zeros((V7X_SC_LANES,), F32)
                        for i in range(V7X_SC_LANES):
                            acc = None
                            for pg in range(ppb):
                                for k in range(lanes_per_row):
                                    v = buf[slot, pg, r0 + i, pl.ds(k * V7X_SC_LANES, V7X_SC_LANES)]
                                    acc = v if acc is None else acc + v
                            vec = jnp.where(lane == i, jnp.sum(acc), vec)
                        orow[pl.ds(r0, V7X_SC_LANES)] = vec

                    pltpu.sync_copy(orow, out_hbm.at[pl.ds((sq * steps + n) * SC_PIECE_ROWS, SC_PIECE_ROWS)])

    sums = pl.kernel(
        body,
        out_type=jax.ShapeDtypeStruct((bsz * steps * SC_PIECE_ROWS,), F32),
        mesh=plsc.VectorSubcoreMesh(core_axis_name="c", subcore_axis_name="s"),
        scratch_types=[pltpu.VMEM((steps * ppb,), jnp.int32),
                       pltpu.VMEM((2, ppb, SC_PIECE_ROWS, V7X_LANES), F32),
                       pltpu.VMEM((SC_PIECE_ROWS,), F32),
                       pltpu.SemaphoreType.DMA((2,))],
        compiler_params=pltpu.CompilerParams(needs_layout_passes=False),
        name="sc_block_sums",
    )(idx, table)
    return sums.reshape(bsz, nblk, ATTN_W)


def _mlp_kernel(x_ref, a_ref, y_ref, sga_ref, sgc_ref, wba_ref, wbc_ref, wo_ref, g_ref, wup_ref, wdn_ref,
                o_ref, *, chunk):
    a = jnp.dot(a_ref[...], wba_ref[...], preferred_element_type=F32)
    c = jnp.dot(y_ref[...], wbc_ref[...], preferred_element_type=F32)
    mixed = sga_ref[...].astype(F32) * a + sgc_ref[...].astype(F32) * c
    x1 = x_ref[...] + jnp.dot(mixed.astype(BF16), wo_ref[...], preferred_element_type=F32)
    h2 = _rms_rows(x1, g_ref[...]).astype(BF16)
    acc = x1
    for c0 in range(0, wup_ref.shape[1], chunk):
        hid = jnp.maximum(jnp.dot(h2, wup_ref[:, c0:c0 + chunk], preferred_element_type=F32), 0.0)
        acc = acc + jnp.dot((hid * hid).astype(BF16), wdn_ref[c0:c0 + chunk, :], preferred_element_type=F32)
    o_ref[...] = acc


def _merge_mlp(x2d, attn, y, sga, sgc, wba, wbc, wo, g_mlp, wup, wdn):
    n, d = x2d.shape
    cfg = _tiles(n)
    tm = cfg["tm"]
    row = lambda w: pl.BlockSpec((tm, w), lambda i: (i, 0))
    weights = [wba, wbc, wo, g_mlp, wup, wdn]
    return pl.pallas_call(
        functools.partial(_mlp_kernel, chunk=cfg["mlp_chunk"]),
        grid=(n // tm,),
        in_specs=[row(d), row(attn.shape[1]), row(d), row(d), row(d)] + [_resident(w.shape) for w in weights],
        out_specs=row(d),
        out_shape=jax.ShapeDtypeStruct((n, d), F32),
        compiler_params=pltpu.CompilerParams(dimension_semantics=("arbitrary",),
                                             vmem_limit_bytes=cfg["mlp_vmem"]),
        name="merge_mlp",
    )(x2d, attn, y, sga, sgc, *weights)


def _heads_last(xT, lead):
    n = xT.shape[1]
    return jnp.transpose(xT.reshape(N_HEADS, HEAD_DIM, n), (2, 0, 1)).reshape(*lead, N_HEADS, HEAD_DIM)


def kernel(x_prompt, x_sample, cache_k, cache_v, state_conv, page_table, norm_mix, w_in, q_norm, k_norm, conv_w, w_br_attn, w_br_conv, w_out, norm_mlp, w_up, w_down):
    bp, t, d = x_prompt.shape
    db, s, _ = x_sample.shape
    depth = w_in.shape[0]
    n_pool, page = cache_k.shape[1], cache_k.shape[2]
    assert bp == 1 and page == V7X_LANES and t % MOBA_BLOCK == 0
    past = page_table.shape[1] * page
    inv = (ROPE_THETA ** (-jnp.arange(ROT_HALF, dtype=F32) * 2.0 / ROT_DIM)).reshape(ROT_HALF, 1)

    ckT_l = [jnp.transpose(cache_k[l], (0, 2, 3, 1)).reshape(n_pool, ATTN_W, page) for l in range(depth)]
    cvT_l = [jnp.transpose(cache_v[l], (0, 2, 3, 1)).reshape(n_pool, ATTN_W, page) for l in range(depth)]
    ksum = [_sc_block_sums(page_table, ckT_l[l]) for l in range(depth)]

    xp = x_prompt.reshape(t, d)
    xs = x_sample.reshape(db * s, d)
    kp_l, vp_l, cp_l, ks_l, vs_l, cs_l = [], [], [], [], [], []
    for l in range(depth):
        w_qkvT = w_in[l][:, :3 * ATTN_W].T.astype(BF16)
        w_rest = w_in[l][:, 3 * ATTN_W:].astype(BF16)
        g_mix = norm_mix[l].reshape(1, d)
        qg = q_norm[l].reshape(HEAD_DIM, 1)
        kg = k_norm[l].reshape(HEAD_DIM, 1)
        cw = conv_w[l]
        weights = (w_br_attn[l].astype(BF16), w_br_conv[l].astype(BF16), w_out[l].astype(BF16),
                   norm_mlp[l].reshape(1, d), w_up[l].astype(BF16), w_down[l].astype(BF16))

        qT, kT, vT, y, sga, sgc, u_tail, k3, v3, km = _project(xp, g_mix, w_qkvT, w_rest, qg, kg, inv, cw, pos0=0)
        attn = _prompt_attention(qT, km.reshape(t // MOBA_BLOCK, ATTN_W), k3, v3)
        xp = _merge_mlp(xp, attn, y, sga, sgc, *weights)
        kp_l.append(_heads_last(kT, (bp, t)))
        vp_l.append(_heads_last(vT, (bp, t)))
        cp_l.append(u_tail[V7X_SUBLANES - (CONV_K - 1):].reshape(bp, CONV_K - 1, d))

        st = state_conv[l]
        zero = jnp.zeros((db, s, d), F32)
        s1 = zero.at[:, 0].set(st[:, 1]).reshape(db * s, d)
        s2 = zero.at[:, 0].set(st[:, 0]).at[:, 1].set(st[:, 1]).reshape(db * s, d)
        qT, kT, vT, y, sga, sgc, u = _project(xs, g_mix, w_qkvT, w_rest, qg, kg, inv, cw, pos0=past,
                                               prev=(s1, s2, s))
        oT = _decode_attention(page_table, qT, kT, vT, ksum[l], ckT_l[l], cvT_l[l], seq=s)
        attn = jnp.transpose(oT, (0, 2, 1)).reshape(db * s, ATTN_W).astype(BF16)
        xs = _merge_mlp(xs, attn, y, sga, sgc, *weights)
        ks_l.append(_heads_last(kT, (db, s)))
        vs_l.append(_heads_last(vT, (db, s)))
        cs_l.append(u.reshape(db, s, d)[:, s - (CONV_K - 1):])

    return (xp.reshape(bp, t, d), xs.reshape(db, s, d), jnp.stack(kp_l), jnp.stack(vp_l), jnp.stack(cp_l),
            jnp.stack(ks_l), jnp.stack(vs_l), jnp.stack(cs_l))
```
